```python
import math
import jax, jax.numpy as jnp
from jax import lax
import numpy as np

D_MODEL = 1024
BATCH = 8
SEQ = 8192
DEPTH = 1

N_META = 16
SSD_EXPAND = 2
D_INNER = SSD_EXPAND * D_MODEL
SSD_HEADDIM = 64
SSD_HEADS = D_INNER // SSD_HEADDIM
SSD_GROUPS = 4
SSD_HPG = SSD_HEADS // SSD_GROUPS
D_STATE = 128
CONV_K = 4
CONV_DIM = D_INNER + 2 * SSD_GROUPS * D_STATE
CHUNK = 256
DT_MIN = 1e-3
DT_MAX = 1e-1
MLA_HEADS = 16
QK_NOPE = 64
QK_ROPE = 32
V_HEAD = 64
Q_LORA = 384
KV_LORA = 256
ROPE_THETA = 10000.0
Q_BLOCK = 128
N_EXPERTS = 32
TOP_K = 4
D_EXPERT = D_MODEL
SWIGLU_ALPHA = 1.702
SWIGLU_LIMIT = 7.0
MOE_BLOCK = 256
EPS = 1e-5
MLA_EPS = 1e-6

IN_SPLITS = [
    D_INNER,
    D_INNER + CONV_DIM,
    D_INNER + CONV_DIM + SSD_HEADS,
    D_INNER + CONV_DIM + SSD_HEADS + Q_LORA,
    D_INNER + CONV_DIM + SSD_HEADS + Q_LORA + KV_LORA + QK_ROPE,
]
IN_COLS = IN_SPLITS[-1] + 2 * D_MODEL

kernel_name = "meta_ssd_mla_gated_moe_block"


def rms_norm(x, g, eps=EPS):
    xf = x.astype(jnp.float32)
    y = xf * lax.rsqrt(jnp.mean(xf * xf, axis=-1, keepdims=True) + eps)
    return (y * g.astype(jnp.float32)).astype(x.dtype)


def rope_tables(n_pos, dim):
    pos = jnp.arange(n_pos, dtype=jnp.float32)
    inv = ROPE_THETA ** (-jnp.arange(0, dim, 2, dtype=jnp.float32) / dim)
    ang = pos[:, None] * inv[None, :]
    return jnp.cos(ang), jnp.sin(ang)


def apply_rope(x, cos, sin):
    x1, x2 = jnp.split(x.astype(jnp.float32), 2, axis=-1)
    return jnp.concatenate([x1 * cos - x2 * sin, x2 * cos + x1 * sin], axis=-1).astype(x.dtype)


def causal_depthwise_conv(x, w, b):
    y = lax.conv_general_dilated(
        x, w[:, None, :].astype(x.dtype), window_strides=(1,), padding=[(CONV_K - 1, 0)],
        dimension_numbers=("NWC", "WIO", "NWC"), feature_group_count=x.shape[-1])
    return y + b.astype(x.dtype)


def ssd_scan(xh, dt, a, bm, cm):
    bsz, t = xh.shape[:2]
    nc = t // CHUNK
    xc = xh.reshape(bsz, nc, CHUNK, SSD_GROUPS, SSD_HPG, SSD_HEADDIM)
    dtc = dt.reshape(bsz, nc, CHUNK, SSD_GROUPS, SSD_HPG)
    bc = bm.reshape(bsz, nc, CHUNK, SSD_GROUPS, D_STATE)
    cc = cm.reshape(bsz, nc, CHUNK, SSD_GROUPS, D_STATE)
    a_cs = jnp.cumsum(jnp.moveaxis(dtc * a, 2, -1), axis=-1)
    xdt = xc * dtc[..., None]
    causal = jnp.tril(jnp.ones((CHUNK, CHUNK), dtype=bool))
    decay = jnp.exp(jnp.where(causal, a_cs[..., :, None] - a_cs[..., None, :], -jnp.inf))
    cb = jnp.einsum("bclgn,bcsgn->bcgls", cc, bc)
    scores = cb[:, :, :, None] * decay
    y_diag = jnp.einsum("bcgrls,bcsgrp->bclgrp", scores, xdt)
    decay_states = jnp.moveaxis(jnp.exp(a_cs[..., -1:] - a_cs), -1, 2)
    states = jnp.einsum("bclgn,bclgrp->bcgrpn", bc, xdt * decay_states[..., None])
    chunk_decay = jnp.exp(a_cs[..., -1])

    def step(s, inp):
        st, dec = inp
        return dec[..., None, None] * s + st, s

    s0 = jnp.zeros_like(states[:, 0])
    _, prev = lax.scan(step, s0, (jnp.moveaxis(states, 1, 0), jnp.moveaxis(chunk_decay, 1, 0)))
    prev = jnp.moveaxis(prev, 0, 1)
    state_decay = jnp.moveaxis(jnp.exp(a_cs), -1, 2)
    y_off = jnp.einsum("bclgn,bcgrpn->bclgrp", cc, prev) * state_decay[..., None]
    return (y_diag + y_off).reshape(bsz, t, SSD_GROUPS, SSD_HPG, SSD_HEADDIM)


def ssd_branch(z, xbc, dt_raw, conv_w, conv_b, dt_bias, a_log, d_skip, norm_w):
    bsz, l = z.shape[:2]
    xbc = jax.nn.silu(causal_depthwise_conv(xbc, conv_w, conv_b))
    xs, bm, cm = jnp.split(xbc, [D_INNER, D_INNER + SSD_GROUPS * D_STATE], axis=-1)
    dt = jax.nn.softplus(dt_raw.astype(jnp.float32) + dt_bias.astype(jnp.float32))
    a = -jnp.exp(a_log.astype(jnp.float32)).reshape(SSD_GROUPS, SSD_HPG)
    t_real = l - N_META
    n_real_chunks = -(-t_real // CHUNK)
    front = CHUNK - N_META
    pad = ((0, 0), (front, n_real_chunks * CHUNK - t_real), (0, 0))
    xh = jnp.pad(xs, pad).reshape(bsz, -1, SSD_GROUPS, SSD_HPG, SSD_HEADDIM)
    bp = jnp.pad(bm, pad).reshape(bsz, -1, SSD_GROUPS, D_STATE)
    cp = jnp.pad(cm, pad).reshape(bsz, -1, SSD_GROUPS, D_STATE)
    dtp = jnp.pad(dt, pad).reshape(bsz, -1, SSD_GROUPS, SSD_HPG)
    y = ssd_scan(xh, dtp, a, bp, cp)[:, front:front + l]
    y = y + d_skip.astype(jnp.float32).reshape(SSD_GROUPS, SSD_HPG)[:, :, None] * \
        xs.reshape(bsz, l, SSD_GROUPS, SSD_HPG, SSD_HEADDIM)
    yg = (y.reshape(bsz, l, D_INNER) * jax.nn.silu(z.astype(jnp.float32))).astype(jnp.float32)
    yg = yg.reshape(bsz, l, SSD_GROUPS, D_INNER // SSD_GROUPS)
    yg = yg * lax.rsqrt(jnp.mean(yg * yg, axis=-1, keepdims=True) + EPS)
    return (yg.reshape(bsz, l, D_INNER) * norm_w.astype(jnp.float32)).astype(z.dtype)


def mla_branch(q_a, kv_a, q_norm, w_uq, kv_norm, w_ukv, cos, sin):
    bsz, l = q_a.shape[:2]
    scale = (QK_NOPE + QK_ROPE) ** -0.5
    q = (rms_norm(q_a, q_norm, MLA_EPS) @ w_uq).reshape(bsz, l, MLA_HEADS, QK_NOPE + QK_ROPE)
    q_nope, q_pe = jnp.split(q, [QK_NOPE], axis=-1)
    q_pe = apply_rope(q_pe, cos[:, None], sin[:, None])
    c_kv, k_pe = jnp.split(kv_a, [KV_LORA], axis=-1)
    k_pe = apply_rope(k_pe, cos, sin)
    kv = (rms_norm(c_kv, kv_norm, MLA_EPS) @ w_ukv).reshape(bsz, l, MLA_HEADS, QK_NOPE + V_HEAD)
    k_nope, v = jnp.split(kv, [QK_NOPE], axis=-1)
    n_blocks = -(-l // Q_BLOCK)
    lp = n_blocks * Q_BLOCK
    qpad = ((0, 0), (0, lp - l), (0, 0), (0, 0))
    qn_b = jnp.moveaxis(jnp.pad(q_nope * scale, qpad).reshape(bsz, n_blocks, Q_BLOCK, MLA_HEADS, QK_NOPE), 1, 0)
    qp_b = jnp.moveaxis(jnp.pad(q_pe * scale, qpad).reshape(bsz, n_blocks, Q_BLOCK, MLA_HEADS, QK_ROPE), 1, 0)
    k_idx = jnp.arange(l)

    def attend(blk):
        qn, qp, start = blk
        s = jnp.einsum("bqhd,bkhd->bhqk", qn, k_nope) + jnp.einsum("bqhr,bkr->bhqk", qp, k_pe)
        q_idx = start + jnp.arange(Q_BLOCK)
        s = jnp.where(k_idx[None, :] <= q_idx[:, None], s.astype(jnp.float32), -jnp.inf)
        p = jax.nn.softmax(s, axis=-1).astype(v.dtype)
        return jnp.einsum("bhqk,bkhd->bqhd", p, v)

    starts = jnp.arange(n_blocks, dtype=jnp.int32) * Q_BLOCK
    o = lax.map(attend, (qn_b, qp_b, starts))
    return jnp.moveaxis(o, 0, 1).reshape(bsz, lp, MLA_HEADS * V_HEAD)[:, :l]


def clamped_swiglu(h):
    h_glu, h_lin = h[..., ::2], h[..., 1::2]
    h_glu = jnp.minimum(h_glu, SWIGLU_LIMIT)
    h_lin = jnp.clip(h_lin, -SWIGLU_LIMIT, SWIGLU_LIMIT)
    return h_glu * jax.nn.sigmoid(SWIGLU_ALPHA * h_glu) * (h_lin + 1)


def moe_ffn(xn, w_router, b_router, w1, b1, w2, b2):
    bsz, l, d = xn.shape
    tok_x = xn.reshape(-1, d)
    n_tok = tok_x.shape[0]
    logits = (tok_x @ w_router + b_router).astype(jnp.float32)
    top_logit, top_e = lax.top_k(logits, TOP_K)
    top_w = jax.nn.softmax(top_logit, axis=-1)
    n_assign = n_tok * TOP_K
    flat_e = top_e.reshape(-1)
    order = jnp.argsort(flat_e)
    e_sorted = flat_e[order]
    tok_sorted = (order // TOP_K).astype(jnp.int32)
    w_sorted = top_w.reshape(-1)[order]
    group_sizes = jnp.bincount(flat_e, length=N_EXPERTS).astype(jnp.int32)
    padded_sizes = ((group_sizes + MOE_BLOCK - 1) // MOE_BLOCK) * MOE_BLOCK
    padded_ends = jnp.cumsum(padded_sizes)
    padded_starts = padded_ends - padded_sizes
    group_starts = jnp.cumsum(group_sizes) - group_sizes
    dest = padded_starts[e_sorted] + jnp.arange(n_assign, dtype=jnp.int32) - group_starts[e_sorted]
    n_blocks = -(-n_assign // MOE_BLOCK) + N_EXPERTS
    n_rows = n_blocks * MOE_BLOCK
    row_tok = jnp.full((n_rows,), n_tok, jnp.int32).at[dest].set(tok_sorted)
    row_w = jnp.zeros((n_rows,), jnp.float32).at[dest].set(w_sorted)
    block_e = jnp.minimum(jnp.searchsorted(padded_ends, jnp.arange(n_blocks) * MOE_BLOCK, side="right"),
                          N_EXPERTS - 1).astype(jnp.int32)
    x_pad = jnp.concatenate([tok_x, jnp.zeros((1, d), tok_x.dtype)], axis=0)

    def expert_block(blk):
        tok_blk, w_blk, e = blk
        hb = x_pad[tok_blk] @ w1[e] + b1[e]
        ob = clamped_swiglu(hb) @ w2[e] + b2[e]
        return ob * w_blk[:, None].astype(ob.dtype)

    rows = lax.map(expert_block, (row_tok.reshape(n_blocks, MOE_BLOCK), row_w.reshape(n_blocks, MOE_BLOCK), block_e))
    y = jax.ops.segment_sum(rows.reshape(n_rows, d), row_tok, num_segments=n_tok + 1)[:n_tok]
    return y.reshape(bsz, l, d)


def setup_inputs(seed: int = 0) -> dict:
    key = jax.random.key(seed)
    ks = jax.random.split(key, 32)
    f32 = jnp.float32

    def nrm(k, shape, scale):
        return jax.random.normal(k, shape, f32) * scale

    def gain(k, shape):
        return 1.0 + 0.02 * jax.random.normal(k, shape, f32)

    dt0 = jnp.exp(jax.random.uniform(ks[6], (DEPTH, SSD_HEADS), f32, math.log(DT_MIN), math.log(DT_MAX)))
    return {
        "x": nrm(ks[0], (BATCH, SEQ, D_MODEL), 1.0),
        "meta_tokens": nrm(ks[1], (N_META, D_MODEL), 1.0),
        "norm_mix": gain(ks[2], (DEPTH, D_MODEL)),
        "w_in": nrm(ks[3], (DEPTH, D_MODEL, IN_COLS), D_MODEL ** -0.5),
        "conv_w": nrm(ks[4], (DEPTH, CONV_K, CONV_DIM), CONV_K ** -0.5),
        "conv_b": nrm(ks[5], (DEPTH, CONV_DIM), 0.02),
        "dt_bias": dt0 + jnp.log(-jnp.expm1(-dt0)),
        "a_log": jnp.log(jax.random.uniform(ks[7], (DEPTH, SSD_HEADS), f32, 1.0, 16.0)),
        "d_skip": gain(ks[8], (DEPTH, SSD_HEADS)),
        "ssd_norm": gain(ks[9], (DEPTH, D_INNER)),
        "w_ssd_out": nrm(ks[10], (DEPTH, D_INNER, D_MODEL), D_INNER ** -0.5),
        "q_norm": gain(ks[11], (DEPTH, Q_LORA)),
        "w_uq": nrm(ks[12], (DEPTH, Q_LORA, MLA_HEADS * (QK_NOPE + QK_ROPE)), Q_LORA ** -0.5),
        "kv_norm": gain(ks[13], (DEPTH, KV_LORA)),
        "w_ukv": nrm(ks[14], (DEPTH, KV_LORA, MLA_HEADS * (QK_NOPE + V_HEAD)), KV_LORA ** -0.5),
        "w_o": nrm(ks[15], (DEPTH, MLA_HEADS * V_HEAD, D_MODEL), (MLA_HEADS * V_HEAD) ** -0.5),
        "w_out": nrm(ks[16], (DEPTH, D_MODEL, D_MODEL), D_MODEL ** -0.5),
        "norm_ffn": gain(ks[17], (DEPTH, D_MODEL)),
        "w_router": nrm(ks[18], (DEPTH, D_MODEL, N_EXPERTS), D_MODEL ** -0.5),
        "b_router": nrm(ks[19], (DEPTH, N_EXPERTS), 0.01),
        "w_mlp1": nrm(ks[20], (DEPTH, N_EXPERTS, D_MODEL, 2 * D_EXPERT), D_MODEL ** -0.5),
        "b_mlp1": nrm(ks[21], (DEPTH, N_EXPERTS, 2 * D_EXPERT), 0.02),
        "w_mlp2": nrm(ks[22], (DEPTH, N_EXPERTS, D_EXPERT, D_MODEL), D_EXPERT ** -0.5),
        "b_mlp2": nrm(ks[23], (DEPTH, N_EXPERTS, D_MODEL), 0.02),
        "norm_final": gain(ks[24], (D_MODEL,)),
    }


def reference(x, meta_tokens, norm_mix, w_in, conv_w, conv_b, dt_bias, a_log, d_skip, ssd_norm, w_ssd_out,
              q_norm, w_uq, kv_norm, w_ukv, w_o, w_out, norm_ffn, w_router, b_router,
              w_mlp1, b_mlp1, w_mlp2, b_mlp2, norm_final):
    bsz = x.shape[0]
    meta = jnp.broadcast_to(meta_tokens[None].astype(x.dtype), (bsz, N_META, D_MODEL))
    h = jnp.concatenate([meta, x], axis=1)
    l = h.shape[1]
    cos, sin = rope_tables(l, QK_ROPE)
    for i in range(DEPTH):
        hn = rms_norm(h, norm_mix[i])
        u = hn @ w_in[i]
        z, xbc, dt_raw, q_a, kv_a, gate_logits = jnp.split(u, IN_SPLITS, axis=-1)
        y_ssd = ssd_branch(z, xbc, dt_raw, conv_w[i], conv_b[i], dt_bias[i], a_log[i], d_skip[i],
                           ssd_norm[i]) @ w_ssd_out[i]
        y_mla = mla_branch(q_a, kv_a, q_norm[i], w_uq[i], kv_norm[i], w_ukv[i], cos, sin) @ w_o[i]
        g_ssd, g_mla = jnp.split(jax.nn.sigmoid(gate_logits), 2, axis=-1)
        h = h + (g_ssd * y_ssd + g_mla * y_mla).astype(h.dtype) @ w_out[i]
        h = h + moe_ffn(rms_norm(h, norm_ffn[i]), w_router[i], b_router[i], w_mlp1[i], b_mlp1[i],
                        w_mlp2[i], b_mlp2[i]).astype(h.dtype)
    return rms_norm(h, norm_final)[:, N_META:]
```

```python
import functools

import jax
import jax.numpy as jnp
from jax import lax
from jax.experimental import pallas as pl
from jax.experimental.pallas import tpu as pltpu

f32 = jnp.float32
bf16 = jnp.bfloat16

D_MODEL = 1024
N_META = 16
D_INNER = 2048
SSD_HEADDIM = 64
SSD_HEADS = 32
SSD_GROUPS = 4
D_STATE = 128
CONV_K = 4
CONV_DIM = D_INNER + 2 * SSD_GROUPS * D_STATE
CHUNK = 256
MLA_HEADS = 16
QK_NOPE = 64
QK_ROPE = 32
V_HEAD = 64
Q_LORA = 384
KV_LORA = 256
ROPE_THETA = 10000.0
N_EXPERTS = 32
TOP_K = 4
D_EXPERT = 1024
SWIGLU_ALPHA = 1.702
SWIGLU_LIMIT = 7.0
EPS = 1e-5
MLA_EPS = 1e-6

FRONT = CHUNK - N_META
LANES = 128
GROUP_COLS = D_INNER // SSD_GROUPS
NEG = -1e30
TAIL = 8

U_Z = 0
U_X = D_INNER
U_BC = 2 * D_INNER
U_GS = U_BC + 2 * SSD_GROUPS * D_STATE
U_GM = U_GS + D_MODEL
U_LAT = U_GM + D_MODEL
LAT_W = 1024
LAT_KPE = KV_LORA
LAT_QA = 384
U_COLS = U_LAT + LAT_W

VMEM_LIMIT = 56 * 1024 * 1024


def _params(sem):
    return pltpu.CompilerParams(dimension_semantics=sem, vmem_limit_bytes=VMEM_LIMIT)


def _rms(x, g, eps):
    return x * lax.rsqrt(jnp.mean(x * x, axis=-1, keepdims=True) + eps) * g


def _sigmoid(x):
    return 1.0 / (1.0 + jnp.exp(-x))


def _dot(a, b):
    return jnp.dot(a, b, preferred_element_type=f32)


def _split2(v):
    hi = v.astype(bf16)
    lo = (v - hi.astype(f32)).astype(bf16)
    return hi, lo


def _split3(v):
    hi = v.astype(bf16)
    r = v - hi.astype(f32)
    mid = r.astype(bf16)
    lo = (r - mid.astype(f32)).astype(bf16)
    return hi, mid, lo


def _norm_matmul_kernel(x_ref, g_ref, w_ref, o_ref):
    y = _rms(x_ref[...], g_ref[...], EPS).astype(bf16)
    o_ref[...] = _dot(y, w_ref[...]).astype(o_ref.dtype)


def _norm_matmul(x, g, w, tm, tn, out_dtype, name):
    m, k = x.shape
    n = w.shape[1]
    return pl.pallas_call(
        _norm_matmul_kernel,
        grid=(n // tn, m // tm),
        in_specs=[pl.BlockSpec((tm, k), lambda j, i: (i, 0)),
                  pl.BlockSpec((1, k), lambda j, i: (0, 0)),
                  pl.BlockSpec((k, tn), lambda j, i: (0, j))],
        out_specs=pl.BlockSpec((tm, tn), lambda j, i: (i, j)),
        out_shape=jax.ShapeDtypeStruct((m, n), out_dtype),
        compiler_params=_params(("parallel", "parallel")),
        name=name,
    )(x, g, w)


def _ssd_kernel(x_ref, bc_ref, z_ref, dt_ref, cw_ref, cb_ref, dtb_ref, a_ref, dsk_ref, nw_ref, e_ref,
                o_ref, xbuf, state, ybuf):
    c = pl.program_id(1)

    @pl.when(c == 0)
    def _():
        xbuf[0:TAIL, :] = jnp.zeros((TAIL, CONV_DIM), f32)
        state[...] = jnp.zeros_like(state)

    xbuf[TAIL:TAIL + CHUNK, 0:D_INNER] = x_ref[...].astype(f32)
    xbuf[TAIL:TAIL + CHUNK, D_INNER:CONV_DIM] = bc_ref[...].astype(f32)
    conv = cb_ref[...]
    for k in range(CONV_K):
        off = TAIL - (CONV_K - 1) + k
        conv = conv + xbuf[off:off + CHUNK, :] * cw_ref[k:k + 1, :]
    xbuf[0:TAIL, :] = xbuf[CHUNK:CHUNK + TAIL, :]
    conv = conv * _sigmoid(conv)
    xs = conv[:, 0:D_INNER]
    bm = conv[:, D_INNER:D_INNER + SSD_GROUPS * D_STATE]
    cm = conv[:, D_INNER + SSD_GROUPS * D_STATE:CONV_DIM]

    dtl = dt_ref[...] + dtb_ref[...]
    dt = jnp.maximum(dtl, 0.0) + jnp.log1p(jnp.exp(-jnp.abs(dtl)))
    row = lax.broadcasted_iota(jnp.int32, (CHUNK, LANES), 0) + c * CHUNK
    dt = jnp.where(row >= FRONT, dt, 0.0)
    da = dt * a_ref[...]

    ri = lax.broadcasted_iota(jnp.int32, (CHUNK, CHUNK), 0)
    ci = lax.broadcasted_iota(jnp.int32, (CHUNK, CHUNK), 1)
    causal = ri >= ci
    tril = causal.astype(bf16)
    acs = sum(_dot(tril, part) for part in _split3(da))
    acs_t = acs.T
    acs_last = acs[CHUNK - 1:CHUNK, :]

    e_mat = e_ref[...]

    def expand(v):
        hi, lo = _split2(v)
        return _dot(hi, e_mat) + _dot(lo, e_mat)

    dt_x = expand(dt)
    eacs_x = expand(jnp.exp(acs))
    dec_x = expand(jnp.exp(acs_last - acs))
    xdt = xs * dt_x
    xdt_b = xdt.astype(bf16)
    xdec_b = (xdt * dec_x).astype(bf16)
    bm_b = bm.astype(bf16)
    cm_b = cm.astype(bf16)
    bm_t = bm.T.astype(bf16)

    lane = lax.broadcasted_iota(jnp.int32, (CHUNK, LANES), 1)
    first_half = lane < SSD_HEADDIM
    heads_per_group = SSD_HEADS // SSD_GROUPS
    for g in range(SSD_GROUPS):
        ns = slice(g * D_STATE, (g + 1) * D_STATE)
        cb = lax.dot_general(cm_b[:, ns], bm_b[:, ns], (((1,), (1,)), ((), ())),
                             preferred_element_type=f32)
        for jp in range(heads_per_group // 2):
            j = g * (heads_per_group // 2) + jp
            cols = slice(j * LANES, (j + 1) * LANES)
            pair = xdt_b[:, cols]
            ys = []
            for hh in range(2):
                h = 2 * j + hh
                seg = acs[:, h:h + 1] - acs_t[h:h + 1, :]
                decay = jnp.where(causal, jnp.exp(jnp.minimum(seg, 0.0)), 0.0)
                ys.append(_dot((cb * decay).astype(bf16), pair))
            ybuf[:, cols] = jnp.where(first_half, ys[0], ys[1])

    for g in range(SSD_GROUPS):
        ns = slice(g * D_STATE, (g + 1) * D_STATE)
        gc = slice(g * GROUP_COLS, (g + 1) * GROUP_COLS)
        st = state[:, gc]
        y_off = _dot(cm_b[:, ns], st.astype(bf16)) * eacs_x[:, gc]
        ybuf[:, gc] = ybuf[:, gc] + y_off
        state[:, gc] = eacs_x[CHUNK - 1:CHUNK, gc] * st + _dot(bm_t[ns, :], xdec_b[:, gc])

    zf = z_ref[...].astype(f32)
    yg = (ybuf[...] + dsk_ref[...] * xs) * (zf * _sigmoid(zf))
    for g in range(SSD_GROUPS):
        gc = slice(g * GROUP_COLS, (g + 1) * GROUP_COLS)
        blk = yg[:, gc]
        o_ref[:, gc] = (_rms(blk, nw_ref[:, gc], EPS)).astype(o_ref.dtype)


def _ssd(u, dt_raw, conv_w, conv_b, dt_bias_p, a_p, dsk_x, norm_w, e_mat, bsz, n_chunks):
    rows = bsz * n_chunks * CHUNK
    row_map = lambda b, c: b * n_chunks + c
    const = lambda b, c: (0, 0)
    return pl.pallas_call(
        _ssd_kernel,
        grid=(bsz, n_chunks),
        in_specs=[pl.BlockSpec((CHUNK, D_INNER), lambda b, c: (row_map(b, c), U_X // D_INNER)),
                  pl.BlockSpec((CHUNK, CONV_DIM - D_INNER), lambda b, c: (row_map(b, c), U_BC // (CONV_DIM - D_INNER))),
                  pl.BlockSpec((CHUNK, D_INNER), lambda b, c: (row_map(b, c), U_Z // D_INNER)),
                  pl.BlockSpec((CHUNK, LANES), lambda b, c: (row_map(b, c), 0)),
                  pl.BlockSpec((CONV_K, CONV_DIM), const),
                  pl.BlockSpec((1, CONV_DIM), const),
                  pl.BlockSpec((1, LANES), const),
                  pl.BlockSpec((1, LANES), const),
                  pl.BlockSpec((1, D_INNER), const),
                  pl.BlockSpec((1, D_INNER), const),
                  pl.BlockSpec((LANES, D_INNER), const)],
        out_specs=pl.BlockSpec((CHUNK, D_INNER), lambda b, c: (row_map(b, c), 0)),
        out_shape=jax.ShapeDtypeStruct((rows, D_INNER), bf16),
        scratch_shapes=[pltpu.VMEM((TAIL + CHUNK, CONV_DIM), f32),
                        pltpu.VMEM((D_STATE, D_INNER), f32),
                        pltpu.VMEM((CHUNK, D_INNER), f32)],
        compiler_params=_params(("parallel", "arbitrary")),
        name="ssd_scan",
    )(u, u, u, dt_raw, conv_w, conv_b, dt_bias_p, a_p, dsk_x, norm_w, e_mat)


def _mla_proj_kernel(lat_ref, qn_ref, kn_ref, wq_ref, wqs_ref, wk_ref, wv_ref, ek_ref, eks_ref,
                     cq_ref, sq_ref, ck_ref, sk_ref, q_ref, k_ref, v_ref):
    lat = lat_ref[...]
    qa = lat[:, LAT_QA:LAT_QA + Q_LORA].astype(f32)
    qn = _rms(qa, qn_ref[...], MLA_EPS).astype(bf16)
    q_lin = _dot(qn, wq_ref[...])
    q_swp = _dot(qn, wqs_ref[...])
    cn = _rms(lat[:, 0:KV_LORA].astype(f32), kn_ref[...], MLA_EPS).astype(bf16)
    k_lin = _dot(cn, wk_ref[...])
    kpe = lat[:, LAT_KPE:LAT_KPE + QK_ROPE]
    k_rot = _dot(kpe, ek_ref[...]) * ck_ref[...] + _dot(kpe, eks_ref[...]) * sk_ref[...]
    cq = cq_ref[...]
    sq = sq_ref[...]
    for h in range(MLA_HEADS):
        hs = slice(h * LANES, (h + 1) * LANES)
        q_ref[:, hs] = (q_lin[:, hs] * cq + q_swp[:, hs] * sq).astype(q_ref.dtype)
        k_ref[:, hs] = (k_lin[:, hs] + k_rot).astype(k_ref.dtype)
    v_ref[...] = _dot(cn, wv_ref[...]).astype(v_ref.dtype)


def _mla_proj(u, q_norm, kv_norm, wq, wqs, wk, wv, ek, eks, cq, sq, ck, sk, tm, lp):
    rows = u.shape[0]
    tiles_per_seq = lp // tm
    const = lambda i: (0, 0)
    tab = lambda i: (i % tiles_per_seq, 0)
    hw = MLA_HEADS * LANES
    return pl.pallas_call(
        _mla_proj_kernel,
        grid=(rows // tm,),
        in_specs=[pl.BlockSpec((tm, LAT_W), lambda i: (i, U_LAT // LAT_W)),
                  pl.BlockSpec((1, Q_LORA), const),
                  pl.BlockSpec((1, KV_LORA), const),
                  pl.BlockSpec((Q_LORA, hw), const),
                  pl.BlockSpec((Q_LORA, hw), const),
                  pl.BlockSpec((KV_LORA, hw), const),
                  pl.BlockSpec((KV_LORA, MLA_HEADS * V_HEAD), const),
                  pl.BlockSpec((QK_ROPE, LANES), const),
                  pl.BlockSpec((QK_ROPE, LANES), const),
                  pl.BlockSpec((tm, LANES), tab),
                  pl.BlockSpec((tm, LANES), tab),
                  pl.BlockSpec((tm, LANES), tab),
                  pl.BlockSpec((tm, LANES), tab)],
        out_specs=[pl.BlockSpec((tm, hw), lambda i: (i, 0)),
                   pl.BlockSpec((tm, hw), lambda i: (i, 0)),
                   pl.BlockSpec((tm, MLA_HEADS * V_HEAD), lambda i: (i, 0))],
        out_shape=[jax.ShapeDtypeStruct((rows, hw), bf16),
                   jax.ShapeDtypeStruct((rows, hw), bf16),
                   jax.ShapeDtypeStruct((rows, MLA_HEADS * V_HEAD), bf16)],
        compiler_params=_params(("parallel",)),
        name="mla_proj",
    )(u, q_norm, kv_norm, wq, wqs, wk, wv, ek, eks, cq, sq, ck, sk)


def _attn_kernel(q_ref, k_ref, v_ref, o_ref, *, tq, n_q):
    ri = lax.broadcasted_iota(jnp.int32, (tq, tq), 0)
    ci = lax.broadcasted_iota(jnp.int32, (tq, tq), 1)

    def q_block(qi, _):
        q0 = pl.multiple_of(qi * tq, tq)
        for hh in range(2):
            hs = slice(hh * LANES, (hh + 1) * LANES)
            q = q_ref[0, pl.ds(q0, tq), hs]

            def k_step(kj, carry, masked):
                m, l, acc = carry
                k0 = pl.multiple_of(kj * tq, tq)
                k = k_ref[0, pl.ds(k0, tq), hs]
                s = lax.dot_general(q, k, (((1,), (1,)), ((), ())), preferred_element_type=f32)
                if masked:
                    kpos = ci + k0
                    s = jnp.where((kpos <= ri + q0) & (kpos >= FRONT), s, NEG)
                m_new = jnp.maximum(m, jnp.max(s, axis=-1, keepdims=True))
                alpha = jnp.exp(m - m_new)
                p = jnp.exp(s - m_new)
                l = alpha * l + jnp.sum(p, axis=-1, keepdims=True)
                acc = alpha * acc + _dot(p.astype(bf16), v_ref[0, pl.ds(k0, tq), :])
                return m_new, l, acc

            carry = (jnp.full((tq, 1), NEG, f32), jnp.zeros((tq, 1), f32), jnp.zeros((tq, LANES), f32))
            carry = k_step(0, carry, True)
            carry = lax.fori_loop(1, qi, lambda kj, cr: k_step(kj, cr, False), carry)
            carry = lax.cond(qi > 0, lambda cr: k_step(qi, cr, True), lambda cr: cr, carry)
            _, l, acc = carry
            vs = slice(hh * V_HEAD, (hh + 1) * V_HEAD)
            o_ref[0, pl.ds(q0, tq), vs] = (acc[:, vs] / l).astype(o_ref.dtype)
        return 0

    lax.fori_loop(0, n_q, q_block, 0)


def _attention(q, k, v, tq):
    bsz, lp, _ = q.shape
    pairs = MLA_HEADS // 2
    return pl.pallas_call(
        functools.partial(_attn_kernel, tq=tq, n_q=lp // tq),
        grid=(bsz, pairs),
        in_specs=[pl.BlockSpec((1, lp, 2 * LANES), lambda b, j: (b, 0, j)),
                  pl.BlockSpec((1, lp, 2 * LANES), lambda b, j: (b, 0, j)),
                  pl.BlockSpec((1, lp, 2 * V_HEAD), lambda b, j: (b, 0, j))],
        out_specs=pl.BlockSpec((1, lp, 2 * V_HEAD), lambda b, j: (b, 0, j)),
        out_shape=jax.ShapeDtypeStruct((bsz, lp, MLA_HEADS * V_HEAD), bf16),
        compiler_params=_params(("parallel", "parallel")),
        name="mla_attention",
    )(q, k, v)


def _merge_kernel(h_ref, y_ref, a_ref, gs_ref, gm_ref, wso_ref, wo_ref, wout_ref, nf_ref, wr_ref, br_ref,
                  h2_ref, xn_ref, te_ref, tw_ref):
    y_ssd = _dot(y_ref[...], wso_ref[...])
    y_mla = _dot(a_ref[...], wo_ref[...])
    mix = _sigmoid(gs_ref[...].astype(f32)) * y_ssd + _sigmoid(gm_ref[...].astype(f32)) * y_mla
    h2 = h_ref[...] + _dot(mix.astype(bf16), wout_ref[...])
    h2_ref[...] = h2
    xn = _rms(h2, nf_ref[...], EPS)
    xn_ref[...] = xn
    logits = _dot(xn.astype(bf16), wr_ref[...]) + br_ref[...]
    lane = lax.broadcasted_iota(jnp.int32, logits.shape, 1).astype(f32)
    te = jnp.zeros(logits.shape, f32)
    tw = jnp.zeros(logits.shape, f32)
    top = None
    for k in range(TOP_K):
        m = jnp.max(logits, axis=-1, keepdims=True)
        idx = jnp.min(jnp.where(logits == m, lane, float(LANES)), axis=-1, keepdims=True)
        if top is None:
            top = m
        te = jnp.where(lane == float(k), idx, te)
        tw = jnp.where(lane == float(k), jnp.exp(m - top), tw)
        logits = jnp.where(lane == idx, -jnp.inf, logits)
    tw = tw / jnp.sum(tw, axis=-1, keepdims=True)
    te_ref[...] = te.astype(jnp.int32)
    tw_ref[...] = tw


def _merge(h, yn, ao, u, wso, wo, wout, norm_ffn, wr, br, tm):
    rows = h.shape[0]
    const = lambda i: (0, 0)
    row = lambda i: (i, 0)
    return pl.pallas_call(
        _merge_kernel,
        grid=(rows // tm,),
        in_specs=[pl.BlockSpec((tm, D_MODEL), row),
                  pl.BlockSpec((tm, D_INNER), row),
                  pl.BlockSpec((tm, D_MODEL), row),
                  pl.BlockSpec((tm, D_MODEL), lambda i: (i, U_GS // D_MODEL)),
                  pl.BlockSpec((tm, D_MODEL), lambda i: (i, U_GM // D_MODEL)),
                  pl.BlockSpec((D_INNER, D_MODEL), const),
                  pl.BlockSpec((D_MODEL, D_MODEL), const),
                  pl.BlockSpec((D_MODEL, D_MODEL), const),
                  pl.BlockSpec((1, D_MODEL), const),
                  pl.BlockSpec((D_MODEL, LANES), const),
                  pl.BlockSpec((1, LANES), const)],
        out_specs=[pl.BlockSpec((tm, D_MODEL), row),
                   pl.BlockSpec((tm, D_MODEL), row),
                   pl.BlockSpec((tm, LANES), row),
                   pl.BlockSpec((tm, LANES), row)],
        out_shape=[jax.ShapeDtypeStruct((rows, D_MODEL), f32),
                   jax.ShapeDtypeStruct((rows, D_MODEL), f32),
                   jax.ShapeDtypeStruct((rows, LANES), jnp.int32),
                   jax.ShapeDtypeStruct((rows, LANES), f32)],
        compiler_params=_params(("parallel",)),
        name="merge_route",
    )(h, yn, ao, u, u, wso, wo, wout, norm_ffn, wr, br)


def _row_gather(src_hbm, dst, sem, idx_ref, n):
    def body(r, _):
        pltpu.make_async_copy(src_hbm.at[pl.ds(idx_ref[r], 1)], dst.at[pl.ds(r, 1)], sem).start()
        return 0
    lax.fori_loop(0, n, body, 0)


def _row_gather_wait(src_hbm, dst, sem, n):
    pltpu.make_async_copy(src_hbm.at[pl.ds(0, n)], dst, sem).wait()


def _moe_kernel(be_ref, nb_ref, tok_ref, tokn_ref, x_hbm, rw_ref, w1g_ref, w1l_ref, b1g_ref, b1l_ref,
                w2_ref, b2_ref, o_ref, xbuf, sem, *, bm):
    i = pl.program_id(0)
    n_used = nb_ref[0]
    slot = i % 2

    @pl.when(i == 0)
    def _():
        _row_gather(x_hbm, xbuf.at[0], sem.at[0], tok_ref.at[0, 0], bm)

    @pl.when(i + 1 < n_used)
    def _():
        _row_gather(x_hbm, xbuf.at[1 - slot], sem.at[1 - slot], tokn_ref.at[0, 0], bm)

    @pl.when(i < n_used)
    def _():
        _row_gather_wait(x_hbm, xbuf.at[slot], sem.at[slot], bm)
        xb = xbuf[slot].astype(bf16)
        h_glu = _dot(xb, w1g_ref[0]) + b1g_ref[0]
        h_lin = _dot(xb, w1l_ref[0]) + b1l_ref[0]
        h_glu = jnp.minimum(h_glu, SWIGLU_LIMIT)
        h_lin = jnp.clip(h_lin, -SWIGLU_LIMIT, SWIGLU_LIMIT)
        act = h_glu * _sigmoid(SWIGLU_ALPHA * h_glu) * (h_lin + 1.0)
        ob = _dot(act.astype(bf16), w2_ref[0]) + b2_ref[0]
        o_ref[...] = ob * rw_ref[...]

    @pl.when(i >= n_used)
    def _():
        o_ref[...] = jnp.zeros_like(o_ref)


def _moe(block_e, n_used, row_tok, xn, row_w, w1g, w1l, b1g, b1l, w2, b2, bm):
    n_blocks = block_e.shape[0]
    last = n_blocks - 1
    grid_spec = pltpu.PrefetchScalarGridSpec(
        num_scalar_prefetch=2,
        grid=(n_blocks,),
        in_specs=[pl.BlockSpec((1, 1, bm), lambda i, be, nb: (jnp.minimum(i, nb[0] - 1), 0, 0),
                               memory_space=pltpu.SMEM),
                  pl.BlockSpec((1, 1, bm), lambda i, be, nb: (jnp.minimum(i + 1, last), 0, 0),
                               memory_space=pltpu.SMEM),
                  pl.BlockSpec(memory_space=pl.ANY),
                  pl.BlockSpec((bm, 1), lambda i, be, nb: (i, 0)),
                  pl.BlockSpec((1, D_MODEL, D_EXPERT), lambda i, be, nb: (be[i], 0, 0)),
                  pl.BlockSpec((1, D_MODEL, D_EXPERT), lambda i, be, nb: (be[i], 0, 0)),
                  pl.BlockSpec((1, 1, D_EXPERT), lambda i, be, nb: (be[i], 0, 0)),
                  pl.BlockSpec((1, 1, D_EXPERT), lambda i, be, nb: (be[i], 0, 0)),
                  pl.BlockSpec((1, D_EXPERT, D_MODEL), lambda i, be, nb: (be[i], 0, 0)),
                  pl.BlockSpec((1, 1, D_MODEL), lambda i, be, nb: (be[i], 0, 0))],
        out_specs=pl.BlockSpec((bm, D_MODEL), lambda i, be, nb: (i, 0)),
        scratch_shapes=[pltpu.VMEM((2, bm, D_MODEL), f32), pltpu.SemaphoreType.DMA((2,))],
    )
    tok3 = row_tok.reshape(n_blocks, 1, bm)
    return pl.pallas_call(
        functools.partial(_moe_kernel, bm=bm),
        grid_spec=grid_spec,
        out_shape=jax.ShapeDtypeStruct((n_blocks * bm, D_MODEL), f32),
        compiler_params=_params(("arbitrary",)),
        name="moe_experts",
    )(block_e, n_used, tok3, tok3, xn, row_w, w1g, w1l, b1g, b1l, w2, b2)


def _combine_kernel(pos_ref, posn_ref, rows_hbm, h_ref, g_ref, o_ref, gbuf, sem, *, tm, n_steps):
    i = pl.program_id(0)
    slot = i % 2

    @pl.when(i == 0)
    def _():
        _row_gather(rows_hbm, gbuf.at[0], sem.at[0], pos_ref.at[0, 0], TOP_K * tm)

    @pl.when(i + 1 < n_steps)
    def _():
        _row_gather(rows_hbm, gbuf.at[1 - slot], sem.at[1 - slot], posn_ref.at[0, 0], TOP_K * tm)

    _row_gather_wait(rows_hbm, gbuf.at[slot], sem.at[slot], TOP_K * tm)
    y = h_ref[0]
    for k in range(TOP_K):
        y = y + gbuf[slot, k * tm:(k + 1) * tm, :]
    o_ref[0] = _rms(y, g_ref[...], EPS)


def _combine(pos_tiles, moe_rows, h2, norm_final, bsz, seq, lp, tm):
    tiles_per_seq = seq // tm
    n_steps = bsz * tiles_per_seq
    lead = (lp - seq) // tm
    hmap = lambda i: (i // tiles_per_seq, lead + i % tiles_per_seq, 0)
    return pl.pallas_call(
        functools.partial(_combine_kernel, tm=tm, n_steps=n_steps),
        grid=(n_steps,),
        in_specs=[pl.BlockSpec((1, 1, TOP_K * tm), lambda i: (i, 0, 0), memory_space=pltpu.SMEM),
                  pl.BlockSpec((1, 1, TOP_K * tm), lambda i: (jnp.minimum(i + 1, n_steps - 1), 0, 0),
                               memory_space=pltpu.SMEM),
                  pl.BlockSpec(memory_space=pl.ANY),
                  pl.BlockSpec((1, tm, D_MODEL), hmap),
                  pl.BlockSpec((1, D_MODEL), lambda i: (0, 0))],
        out_specs=pl.BlockSpec((1, tm, D_MODEL), lambda i: (i // tiles_per_seq, i % tiles_per_seq, 0)),
        out_shape=jax.ShapeDtypeStruct((bsz, seq, D_MODEL), f32),
        scratch_shapes=[pltpu.VMEM((2, TOP_K * tm, D_MODEL), f32), pltpu.SemaphoreType.DMA((2,))],
        compiler_params=_params(("arbitrary",)),
        name="moe_combine",
    )(pos_tiles, pos_tiles, moe_rows, h2, norm_final)


def _pack_w_in(w):
    o_dt = D_INNER + CONV_DIM
    o_qa = o_dt + SSD_HEADS
    o_kv = o_qa + Q_LORA
    o_g = o_kv + KV_LORA + QK_ROPE
    zeros = lambda n: jnp.zeros((D_MODEL, n), w.dtype)
    main = jnp.concatenate([
        w[:, :o_dt], w[:, o_g:o_g + 2 * D_MODEL],
        w[:, o_kv:o_g], zeros(LAT_QA - (KV_LORA + QK_ROPE)),
        w[:, o_qa:o_kv], zeros(LAT_W - LAT_QA - Q_LORA)], axis=1).astype(bf16)
    w_dt = jnp.concatenate([w[:, o_dt:o_qa], zeros(LANES - SSD_HEADS)], axis=1).astype(bf16)
    return main, w_dt


def _pad_lanes(v, fill=0.0):
    return jnp.concatenate([v, jnp.full((LANES - v.shape[0],), fill, v.dtype)])[None, :]


def _pack_mla(w_uq, w_ukv):
    hd = QK_NOPE + QK_ROPE
    half = QK_ROPE // 2
    wq = w_uq.reshape(Q_LORA, MLA_HEADS, hd)
    pad = jnp.zeros((Q_LORA, MLA_HEADS, LANES - hd), w_uq.dtype)
    wq_lin = jnp.concatenate([wq, pad], axis=-1)
    wq_swp = jnp.concatenate([jnp.zeros_like(wq[..., :QK_NOPE]), wq[..., QK_NOPE + half:], wq[..., QK_NOPE:QK_NOPE + half],
                              pad], axis=-1)
    wkv = w_ukv.reshape(KV_LORA, MLA_HEADS, QK_NOPE + V_HEAD)
    wk = jnp.concatenate([wkv[..., :QK_NOPE], jnp.zeros((KV_LORA, MLA_HEADS, LANES - QK_NOPE), w_ukv.dtype)], axis=-1)
    wv = wkv[..., QK_NOPE:]
    flat = lambda a: a.reshape(a.shape[0], -1).astype(bf16)
    return flat(wq_lin), flat(wq_swp), flat(wk), flat(wv)


def _rope_tables(lp):
    pos = jnp.maximum(jnp.arange(lp, dtype=f32) - FRONT, 0.0)
    inv = ROPE_THETA ** (-jnp.arange(0, QK_ROPE, 2, dtype=f32) / QK_ROPE)
    ang = pos[:, None] * inv[None, :]
    cos, sin = jnp.cos(ang), jnp.sin(ang)
    scale = (QK_NOPE + QK_ROPE) ** -0.5
    ones = jnp.ones((lp, QK_NOPE), f32)
    z_nope = jnp.zeros((lp, QK_NOPE), f32)
    z_pad = jnp.zeros((lp, LANES - QK_NOPE - QK_ROPE), f32)
    cq = jnp.concatenate([ones, cos, cos, z_pad], axis=1) * scale
    sq = jnp.concatenate([z_nope, -sin, sin, z_pad], axis=1) * scale
    ck = jnp.concatenate([z_nope, cos, cos, z_pad], axis=1)
    sk = jnp.concatenate([z_nope, -sin, sin, z_pad], axis=1)
    return cq, sq, ck, sk


def _rope_placement():
    half = QK_ROPE // 2
    r = jnp.arange(QK_ROPE)
    ek = jnp.zeros((QK_ROPE, LANES), f32).at[r, QK_NOPE + r].set(1.0)
    eks = jnp.zeros((QK_ROPE, LANES), f32).at[r, QK_NOPE + (r + half) % QK_ROPE].set(1.0)
    return ek.astype(bf16), eks.astype(bf16)


def _routing_tables(top_e, top_w, valid, bm, n_blocks):
    n_rows = n_blocks * bm
    flat_e = jnp.where(valid[:, None], top_e, N_EXPERTS).reshape(-1)
    onehot = (flat_e[:, None] == jnp.arange(N_EXPERTS, dtype=jnp.int32)[None, :]).astype(jnp.int32)
    csum = jnp.cumsum(onehot, axis=0)
    counts = csum[-1]
    rank = jnp.sum((csum - 1) * onehot, axis=1)
    padded = ((counts + bm - 1) // bm) * bm
    pends = jnp.cumsum(padded)
    pstarts = pends - padded
    dest = jnp.where(flat_e < N_EXPERTS, pstarts[jnp.minimum(flat_e, N_EXPERTS - 1)] + rank, n_rows)
    tok = jnp.arange(flat_e.shape[0], dtype=jnp.int32) // TOP_K
    row_tok = jnp.zeros((n_rows,), jnp.int32).at[dest].set(tok, mode="drop")
    row_w = jnp.zeros((n_rows,), f32).at[dest].set(top_w.reshape(-1), mode="drop")
    block_e = jnp.minimum(jnp.searchsorted(pends, jnp.arange(n_blocks, dtype=jnp.int32) * bm, side="right"),
                          N_EXPERTS - 1).astype(jnp.int32)
    n_used = (pends[-1] // bm).astype(jnp.int32).reshape(1)
    return dest.astype(jnp.int32), row_tok, row_w, block_e, n_used


def kernel(x, meta_tokens, norm_mix, w_in, conv_w, conv_b, dt_bias, a_log, d_skip, ssd_norm, w_ssd_out, q_norm, w_uq,
           kv_norm, w_ukv, w_o, w_out, norm_ffn, w_router, b_router, w_mlp1, b_mlp1, w_mlp2, b_mlp2, norm_final):
    bsz, seq, _ = x.shape
    assert seq % CHUNK == 0 and w_in.shape[0] == 1
    lp = FRONT + N_META + seq
    n_chunks = lp // CHUNK
    rows = bsz * lp
    tm = 512 if rows % 512 == 0 else CHUNK

    h = jnp.concatenate([jnp.zeros((bsz, FRONT, D_MODEL), x.dtype),
                         jnp.broadcast_to(meta_tokens[None].astype(x.dtype), (bsz, N_META, D_MODEL)), x], axis=1)
    h = h.reshape(rows, D_MODEL)

    w_main, w_dt = _pack_w_in(w_in[0])
    u = _norm_matmul(h, norm_mix, w_main, tm, U_COLS // 2, bf16, "in_proj")
    dt_raw = _norm_matmul(h, norm_mix, w_dt, tm, LANES, f32, "in_proj_dt")

    a_p = _pad_lanes(-jnp.exp(a_log[0].astype(f32)))
    head_of_col = jnp.arange(D_INNER) // SSD_HEADDIM
    e_mat = (jnp.arange(LANES)[:, None] == head_of_col[None, :]).astype(bf16)
    dsk_x = d_skip[0].astype(f32)[head_of_col][None, :]
    yn = _ssd(u, dt_raw, conv_w[0], conv_b, _pad_lanes(dt_bias[0].astype(f32)), a_p, dsk_x, ssd_norm, e_mat,
              bsz, n_chunks)

    wq, wqs, wk, wv = _pack_mla(w_uq[0], w_ukv[0])
    ek, eks = _rope_placement()
    cq, sq, ck, sk = _rope_tables(lp)
    q, k, v = _mla_proj(u, q_norm, kv_norm, wq, wqs, wk, wv, ek, eks, cq, sq, ck, sk, CHUNK, lp)
    ao = _attention(q.reshape(bsz, lp, -1), k.reshape(bsz, lp, -1), v.reshape(bsz, lp, -1), CHUNK)
    ao = ao.reshape(rows, MLA_HEADS * V_HEAD)

    wr = jnp.concatenate([w_router[0], jnp.zeros((D_MODEL, LANES - N_EXPERTS), w_router.dtype)], axis=1).astype(bf16)
    br = _pad_lanes(b_router[0].astype(f32), NEG)
    h2, xn, te, tw = _merge(h, yn, ao, u, w_ssd_out[0].astype(bf16), w_o[0].astype(bf16), w_out[0].astype(bf16),
                            norm_ffn, wr, br, tm)

    bm = CHUNK
    n_assign = bsz * (N_META + seq) * TOP_K
    n_blocks = -(-n_assign // bm) + N_EXPERTS
    valid = (jnp.arange(rows, dtype=jnp.int32) % lp) >= FRONT
    dest, row_tok, row_w, block_e, n_used = _routing_tables(te[:, :TOP_K], tw[:, :TOP_K], valid, bm, n_blocks)
    w1 = w_mlp1[0]
    moe_rows = _moe(block_e, n_used, row_tok, xn, row_w[:, None],
                    w1[:, :, 0::2].astype(bf16), w1[:, :, 1::2].astype(bf16),
                    b_mlp1[0][:, None, 0::2], b_mlp1[0][:, None, 1::2],
                    w_mlp2[0].astype(bf16), b_mlp2[0][:, None, :], bm)

    tc = CHUNK
    pos = dest.reshape(bsz, lp, TOP_K)[:, lp - seq:, :]
    pos_tiles = pos.reshape(bsz * (seq // tc), tc, TOP_K).transpose(0, 2, 1).reshape(-1, 1, TOP_K * tc)
    return _combine(pos_tiles, moe_rows, h2.reshape(bsz, lp, D_MODEL), norm_final[None, :], bsz, seq, lp, tc)
```

```python
import functools
import math

import jax
import jax.numpy as jnp
from jax import lax
from jax.experimental import pallas as pl
from jax.experimental.pallas import tpu as pltpu

f32 = jnp.float32
bf16 = jnp.bfloat16

D_MODEL = 1024
N_META = 16
D_INNER = 2048
SSD_HEADDIM = 64
SSD_HEADS = 32
SSD_GROUPS = 4
D_STATE = 128
CONV_K = 4
CONV_DIM = D_INNER + 2 * SSD_GROUPS * D_STATE
CHUNK = 256
MLA_HEADS = 16
QK_NOPE = 64
QK_ROPE = 32
V_HEAD = 64
Q_LORA = 384
KV_LORA = 256
ROPE_THETA = 10000.0
N_EXPERTS = 32
TOP_K = 4
D_EXPERT = 1024
SWIGLU_ALPHA = 1.702
SWIGLU_LIMIT = 7.0
EPS = 1e-5
MLA_EPS = 1e-6

FRONT = CHUNK - N_META
LANES = 128
GROUP_COLS = D_INNER // SSD_GROUPS
NEG = -1e30
TAIL = 8
BIAS_LANE = QK_NOPE + QK_ROPE
ONES_ROW = V_HEAD
ATTN_TQ = 3 * CHUNK
ATTN_TK = CHUNK
DISPATCH_ROWS = 512

U_Z = 0
U_X = D_INNER
U_BC = 2 * D_INNER
U_GS = U_BC + 2 * SSD_GROUPS * D_STATE
U_GM = U_GS + D_MODEL
U_LAT = U_GM + D_MODEL
LAT_W = 1024
LAT_KPE = KV_LORA
LAT_QA = 384
U_COLS = U_LAT + LAT_W

VMEM_LIMIT = 56 * 1024 * 1024


def _params(sem):
    return pltpu.CompilerParams(dimension_semantics=sem, vmem_limit_bytes=VMEM_LIMIT)


def _rms(x, g, eps):
    return x * lax.rsqrt(jnp.mean(x * x, axis=-1, keepdims=True) + eps) * g


def _sigmoid(x):
    return 1.0 / (1.0 + jnp.exp(-x))


def _dot(a, b):
    return jnp.dot(a, b, preferred_element_type=f32)


def _dot_nt(a, b):
    return lax.dot_general(a, b, (((1,), (1,)), ((), ())), preferred_element_type=f32)


def _dot_tn(a, b):
    return lax.dot_general(a, b, (((0,), (0,)), ((), ())), preferred_element_type=f32)


def _split2(v):
    hi = v.astype(bf16)
    lo = (v - hi.astype(f32)).astype(bf16)
    return hi, lo


def _split3(v):
    hi = v.astype(bf16)
    r = v - hi.astype(f32)
    mid = r.astype(bf16)
    lo = (r - mid.astype(f32)).astype(bf16)
    return hi, mid, lo


def _norm_matmul_kernel(x_ref, g_ref, w_ref, o_ref):
    y = _rms(x_ref[...], g_ref[...], EPS).astype(bf16)
    o_ref[...] = _dot(y, w_ref[...]).astype(o_ref.dtype)


def _norm_matmul(x, g, w, tm, tn, out_dtype, name):
    m, k = x.shape
    n = w.shape[1]
    return pl.pallas_call(
        _norm_matmul_kernel,
        grid=(n // tn, m // tm),
        in_specs=[pl.BlockSpec((tm, k), lambda j, i: (i, 0)),
                  pl.BlockSpec((1, k), lambda j, i: (0, 0)),
                  pl.BlockSpec((k, tn), lambda j, i: (0, j))],
        out_specs=pl.BlockSpec((tm, tn), lambda j, i: (i, j)),
        out_shape=jax.ShapeDtypeStruct((m, n), out_dtype),
        compiler_params=_params(("parallel", "parallel")),
        name=name,
    )(x, g, w)


def _ssd_kernel(x_ref, bc_ref, z_ref, dt_ref, cw_ref, cb_ref, dtb_ref, a_ref, dsk_ref, nw_ref, e_ref,
                o_ref, xbuf, state, ybuf):
    c = pl.program_id(1)

    @pl.when(c == 0)
    def _():
        xbuf[0:TAIL, :] = jnp.zeros((TAIL, CONV_DIM), f32)
        state[...] = jnp.zeros_like(state)

    xbuf[TAIL:TAIL + CHUNK, 0:D_INNER] = x_ref[...].astype(f32)
    xbuf[TAIL:TAIL + CHUNK, D_INNER:CONV_DIM] = bc_ref[...].astype(f32)
    conv = cb_ref[...]
    for k in range(CONV_K):
        off = TAIL - (CONV_K - 1) + k
        conv = conv + xbuf[off:off + CHUNK, :] * cw_ref[k:k + 1, :]
    xbuf[0:TAIL, :] = xbuf[CHUNK:CHUNK + TAIL, :]
    conv = conv * _sigmoid(conv)
    xs = conv[:, 0:D_INNER]
    bm = conv[:, D_INNER:D_INNER + SSD_GROUPS * D_STATE]
    cm = conv[:, D_INNER + SSD_GROUPS * D_STATE:CONV_DIM]

    dtl = dt_ref[...] + dtb_ref[...]
    dt = jnp.maximum(dtl, 0.0) + jnp.log1p(jnp.exp(-jnp.abs(dtl)))
    row = lax.broadcasted_iota(jnp.int32, (CHUNK, LANES), 0) + c * CHUNK
    dt = jnp.where(row >= FRONT, dt, 0.0)
    da = dt * a_ref[...]

    ri = lax.broadcasted_iota(jnp.int32, (CHUNK, CHUNK), 0)
    ci = lax.broadcasted_iota(jnp.int32, (CHUNK, CHUNK), 1)
    causal = ri >= ci
    tril = causal.astype(bf16)
    acs = sum(_dot(tril, part) for part in _split3(da))
    acs_t = acs.T
    acs_last = acs[CHUNK - 1:CHUNK, :]

    e_mat = e_ref[...]

    def expand(v):
        hi, lo = _split2(v)
        return _dot(hi, e_mat) + _dot(lo, e_mat)

    dt_x = expand(dt)
    eacs_x = expand(jnp.exp(acs))
    dec_x = expand(jnp.exp(acs_last - acs))
    xdt = xs * dt_x
    xdt_b = xdt.astype(bf16)
    xdec_b = (xdt * dec_x).astype(bf16)
    bm_b = bm.astype(bf16)
    cm_b = cm.astype(bf16)
    bm_t = bm.T.astype(bf16)

    lane = lax.broadcasted_iota(jnp.int32, (CHUNK, LANES), 1)
    first_half = lane < SSD_HEADDIM
    heads_per_group = SSD_HEADS // SSD_GROUPS
    for g in range(SSD_GROUPS):
        ns = slice(g * D_STATE, (g + 1) * D_STATE)
        cb = _dot_nt(cm_b[:, ns], bm_b[:, ns])
        for jp in range(heads_per_group // 2):
            j = g * (heads_per_group // 2) + jp
            cols = slice(j * LANES, (j + 1) * LANES)
            pair = xdt_b[:, cols]
            ys = []
            for hh in range(2):
                h = 2 * j + hh
                seg = acs[:, h:h + 1] - acs_t[h:h + 1, :]
                decay = jnp.where(causal, jnp.exp(jnp.minimum(seg, 0.0)), 0.0)
                ys.append(_dot((cb * decay).astype(bf16), pair))
            ybuf[:, cols] = jnp.where(first_half, ys[0], ys[1])

    for g in range(SSD_GROUPS):
        ns = slice(g * D_STATE, (g + 1) * D_STATE)
        gc = slice(g * GROUP_COLS, (g + 1) * GROUP_COLS)
        st = state[:, gc]
        y_off = _dot(cm_b[:, ns], st.astype(bf16)) * eacs_x[:, gc]
        ybuf[:, gc] = ybuf[:, gc] + y_off
        state[:, gc] = eacs_x[CHUNK - 1:CHUNK, gc] * st + _dot(bm_t[ns, :], xdec_b[:, gc])

    zf = z_ref[...].astype(f32)
    yg = (ybuf[...] + dsk_ref[...] * xs) * (zf * _sigmoid(zf))
    for g in range(SSD_GROUPS):
        gc = slice(g * GROUP_COLS, (g + 1) * GROUP_COLS)
        blk = yg[:, gc]
        o_ref[:, gc] = (_rms(blk, nw_ref[:, gc], EPS)).astype(o_ref.dtype)


def _ssd(u, dt_raw, conv_w, conv_b, dt_bias_p, a_p, dsk_x, norm_w, e_mat, bsz, n_chunks):
    rows = bsz * n_chunks * CHUNK
    row_map = lambda b, c: b * n_chunks + c
    const = lambda b, c: (0, 0)
    return pl.pallas_call(
        _ssd_kernel,
        grid=(bsz, n_chunks),
        in_specs=[pl.BlockSpec((CHUNK, D_INNER), lambda b, c: (row_map(b, c), U_X // D_INNER)),
                  pl.BlockSpec((CHUNK, CONV_DIM - D_INNER), lambda b, c: (row_map(b, c), U_BC // (CONV_DIM - D_INNER))),
                  pl.BlockSpec((CHUNK, D_INNER), lambda b, c: (row_map(b, c), U_Z // D_INNER)),
                  pl.BlockSpec((CHUNK, LANES), lambda b, c: (row_map(b, c), 0)),
                  pl.BlockSpec((CONV_K, CONV_DIM), const),
                  pl.BlockSpec((1, CONV_DIM), const),
                  pl.BlockSpec((1, LANES), const),
                  pl.BlockSpec((1, LANES), const),
                  pl.BlockSpec((1, D_INNER), const),
                  pl.BlockSpec((1, D_INNER), const),
                  pl.BlockSpec((LANES, D_INNER), const)],
        out_specs=pl.BlockSpec((CHUNK, D_INNER), lambda b, c: (row_map(b, c), 0)),
        out_shape=jax.ShapeDtypeStruct((rows, D_INNER), bf16),
        scratch_shapes=[pltpu.VMEM((TAIL + CHUNK, CONV_DIM), f32),
                        pltpu.VMEM((D_STATE, D_INNER), f32),
                        pltpu.VMEM((CHUNK, D_INNER), f32)],
        compiler_params=_params(("parallel", "arbitrary")),
        name="ssd_scan",
    )(u, u, u, dt_raw, conv_w, conv_b, dt_bias_p, a_p, dsk_x, norm_w, e_mat)


def _mla_proj_kernel(lat_ref, qn_ref, kn_ref, wq_ref, wqs_ref, wk_ref, wvt_ref, ek_ref, eks_ref,
                     cq_ref, sq_ref, ck_ref, sk_ref, qb_ref, kb_ref, vb_ref, q_ref, k_ref, vt_ref):
    lat = lat_ref[...]
    qa = lat[:, LAT_QA:LAT_QA + Q_LORA].astype(f32)
    qn = _rms(qa, qn_ref[...], MLA_EPS).astype(bf16)
    q_lin = _dot(qn, wq_ref[...])
    q_swp = _dot(qn, wqs_ref[...])
    cn = _rms(lat[:, 0:KV_LORA].astype(f32), kn_ref[...], MLA_EPS).astype(bf16)
    k_lin = _dot(cn, wk_ref[...])
    kpe = lat[:, LAT_KPE:LAT_KPE + QK_ROPE]
    k_rot = _dot(kpe, ek_ref[...]) * ck_ref[...] + _dot(kpe, eks_ref[...]) * sk_ref[...] + kb_ref[...]
    cq = cq_ref[...]
    sq = sq_ref[...]
    qb = qb_ref[...]
    for h in range(MLA_HEADS):
        hs = slice(h * LANES, (h + 1) * LANES)
        q_ref[:, hs] = (q_lin[:, hs] * cq + q_swp[:, hs] * sq + qb).astype(q_ref.dtype)
        k_ref[:, hs] = (k_lin[:, hs] + k_rot).astype(k_ref.dtype)
    vt_ref[...] = (_dot_nt(wvt_ref[...], cn) + vb_ref[...]).astype(vt_ref.dtype)


def _mla_proj(u, q_norm, kv_norm, wq, wqs, wk, wvt, ek, eks, cq, sq, ck, sk, qb, kb, vb, tm, lp):
    rows = u.shape[0]
    tiles_per_seq = lp // tm
    const = lambda i: (0, 0)
    tab = lambda i: (i % tiles_per_seq, 0)
    hw = MLA_HEADS * LANES
    return pl.pallas_call(
        _mla_proj_kernel,
        grid=(rows // tm,),
        in_specs=[pl.BlockSpec((tm, LAT_W), lambda i: (i, U_LAT // LAT_W)),
                  pl.BlockSpec((1, Q_LORA), const),
                  pl.BlockSpec((1, KV_LORA), const),
                  pl.BlockSpec((Q_LORA, hw), const),
                  pl.BlockSpec((Q_LORA, hw), const),
                  pl.BlockSpec((KV_LORA, hw), const),
                  pl.BlockSpec((hw, KV_LORA), const),
                  pl.BlockSpec((QK_ROPE, LANES), const),
                  pl.BlockSpec((QK_ROPE, LANES), const),
                  pl.BlockSpec((tm, LANES), tab),
                  pl.BlockSpec((tm, LANES), tab),
                  pl.BlockSpec((tm, LANES), tab),
                  pl.BlockSpec((tm, LANES), tab),
                  pl.BlockSpec((1, LANES), const),
                  pl.BlockSpec((tm, LANES), tab),
                  pl.BlockSpec((hw, 1), const)],
        out_specs=[pl.BlockSpec((tm, hw), lambda i: (i, 0)),
                   pl.BlockSpec((tm, hw), lambda i: (i, 0)),
                   pl.BlockSpec((hw, tm), lambda i: (0, i))],
        out_shape=[jax.ShapeDtypeStruct((rows, hw), bf16),
                   jax.ShapeDtypeStruct((rows, hw), bf16),
                   jax.ShapeDtypeStruct((hw, rows), bf16)],
        compiler_params=_params(("parallel",)),
        name="mla_proj",
    )(u, q_norm, kv_norm, wq, wqs, wk, wvt, ek, eks, cq, sq, ck, sk, qb, kb, vb)


def _attn_kernel(q_ref, k_ref, vt_ref, o_ref, m_ref, acc_ref, s_buf, p_buf, a_buf, *, tq, tk, n_qb):
    sub = tq // tk
    ki = lax.broadcasted_iota(jnp.int32, (tk, tq), 0)
    qi = lax.broadcasted_iota(jnp.int32, (tk, tq), 1)
    heads = [slice(hh * LANES, (hh + 1) * LANES) for hh in range(2)]

    def q_block(qb, n_un):
        q0 = pl.multiple_of(qb * tq, tq)
        for hh in range(2):
            m_ref[hh] = jnp.full((1, tq), NEG, f32)
            acc_ref[hh] = jnp.zeros((LANES, tq), f32)

        def stage_scores(t):
            k0 = pl.multiple_of(t * tk, tk)
            for hh, hs in enumerate(heads):
                s_buf[hh] = _dot_nt(k_ref[pl.ds(k0, tk), hs], q_ref[pl.ds(q0, tq), hs])

        def stage_softmax(t, masked):
            k0 = t * tk
            for hh in range(2):
                st = s_buf[hh]
                if masked:
                    st = jnp.where(ki + k0 <= qi + q0, st, NEG)
                m_old = m_ref[hh]
                m_new = jnp.maximum(m_old, jnp.max(st, axis=0, keepdims=True))
                p_buf[hh] = jnp.exp2((st - m_new).astype(bf16))
                a_buf[hh] = jnp.exp2(m_old - m_new)
                m_ref[hh] = m_new

        def stage_values(t):
            k0 = pl.multiple_of(t * tk, tk)
            for hh, hs in enumerate(heads):
                acc_ref[hh] = a_buf[hh] * acc_ref[hh] + _dot(vt_ref[hs, pl.ds(k0, tk)], p_buf[hh])

        def step(t, values, softmax, masked, scores):
            if values:
                stage_values(t - 2)
            if softmax:
                stage_softmax(t - 1, masked)
            if scores:
                stage_scores(t)

        if isinstance(n_un, int):
            for t in range(sub + 2):
                step(t, 0 <= t - 2, 0 <= t - 1 < sub, True, t < sub)
        else:
            step(0, False, False, False, True)
            step(1, False, True, True, True)

            def steady(t, carry):
                step(t, True, True, False, True)
                return carry

            lax.fori_loop(2, n_un + 1, steady, 0)
            for e in range(1, sub + 2):
                step(n_un + e, True, e <= sub, True, e <= sub - 1)
        for hh in range(2):
            acc = acc_ref[hh]
            out = acc[0:V_HEAD, :] / acc[ONES_ROW:ONES_ROW + 1, :]
            o_ref[hh * V_HEAD:(hh + 1) * V_HEAD, pl.ds(q0, tq)] = out.astype(o_ref.dtype)

    q_block(0, 0)

    def q_loop(qb, carry):
        q_block(qb, qb * sub)
        return carry

    lax.fori_loop(1, n_qb, q_loop, 0)


def _attention(q, k, vt, bsz, lp):
    pairs = MLA_HEADS // 2
    tq = ATTN_TQ if lp % ATTN_TQ == 0 else ATTN_TK
    return pl.pallas_call(
        functools.partial(_attn_kernel, tq=tq, tk=ATTN_TK, n_qb=lp // tq),
        grid=(bsz, pairs),
        in_specs=[pl.BlockSpec((lp, 2 * LANES), lambda b, j: (b, j)),
                  pl.BlockSpec((lp, 2 * LANES), lambda b, j: (b, j)),
                  pl.BlockSpec((2 * LANES, lp), lambda b, j: (j, b))],
        out_specs=pl.BlockSpec((2 * V_HEAD, lp), lambda b, j: (j, b)),
        out_shape=jax.ShapeDtypeStruct((MLA_HEADS * V_HEAD, bsz * lp), bf16),
        scratch_shapes=[pltpu.VMEM((2, 1, tq), f32), pltpu.VMEM((2, LANES, tq), f32),
                        pltpu.VMEM((2, ATTN_TK, tq), f32), pltpu.VMEM((2, ATTN_TK, tq), bf16),
                        pltpu.VMEM((2, 1, tq), f32)],
        compiler_params=_params(("parallel", "parallel")),
        name="mla_attention",
    )(q, k, vt)


def _merge_kernel(h_ref, y_ref, at_ref, gs_ref, gm_ref, wso_ref, wo_ref, wout_ref, nf_ref, wr_ref, br_ref,
                  h2_ref, xn_ref, te_ref, tw_ref):
    y_ssd = _dot(y_ref[...], wso_ref[...])
    y_mla = _dot_tn(at_ref[...], wo_ref[...])
    mix = _sigmoid(gs_ref[...].astype(f32)) * y_ssd + _sigmoid(gm_ref[...].astype(f32)) * y_mla
    h2 = h_ref[...] + _dot(mix.astype(bf16), wout_ref[...])
    h2_ref[...] = h2
    xn = _rms(h2, nf_ref[...], EPS)
    xn_ref[...] = xn
    logits = _dot(xn.astype(bf16), wr_ref[...]) + br_ref[...]
    lane = lax.broadcasted_iota(jnp.int32, logits.shape, 1).astype(f32)
    te = jnp.zeros(logits.shape, f32)
    tw = jnp.zeros(logits.shape, f32)
    top = None
    for k in range(TOP_K):
        m = jnp.max(logits, axis=-1, keepdims=True)
        idx = jnp.min(jnp.where(logits == m, lane, float(LANES)), axis=-1, keepdims=True)
        if top is None:
            top = m
        te = jnp.where(lane == float(k), idx, te)
        tw = jnp.where(lane == float(k), jnp.exp(m - top), tw)
        logits = jnp.where(lane == idx, -jnp.inf, logits)
    tw = tw / jnp.sum(tw, axis=-1, keepdims=True)
    te_ref[...] = te.astype(jnp.int32)
    tw_ref[...] = tw


def _merge(h, yn, aot, u, wso, wo, wout, norm_ffn, wr, br, tm):
    rows = h.shape[0]
    const = lambda i: (0, 0)
    row = lambda i: (i, 0)
    return pl.pallas_call(
        _merge_kernel,
        grid=(rows // tm,),
        in_specs=[pl.BlockSpec((tm, D_MODEL), row),
                  pl.BlockSpec((tm, D_INNER), row),
                  pl.BlockSpec((MLA_HEADS * V_HEAD, tm), lambda i: (0, i)),
                  pl.BlockSpec((tm, D_MODEL), lambda i: (i, U_GS // D_MODEL)),
                  pl.BlockSpec((tm, D_MODEL), lambda i: (i, U_GM // D_MODEL)),
                  pl.BlockSpec((D_INNER, D_MODEL), const),
                  pl.BlockSpec((D_MODEL, D_MODEL), const),
                  pl.BlockSpec((D_MODEL, D_MODEL), const),
                  pl.BlockSpec((1, D_MODEL), const),
                  pl.BlockSpec((D_MODEL, LANES), const),
                  pl.BlockSpec((1, LANES), const)],
        out_specs=[pl.BlockSpec((tm, D_MODEL), row),
                   pl.BlockSpec((tm, D_MODEL), row),
                   pl.BlockSpec((tm, LANES), row),
                   pl.BlockSpec((tm, LANES), row)],
        out_shape=[jax.ShapeDtypeStruct((rows, D_MODEL), f32),
                   jax.ShapeDtypeStruct((rows, D_MODEL), f32),
                   jax.ShapeDtypeStruct((rows, LANES), jnp.int32),
                   jax.ShapeDtypeStruct((rows, LANES), f32)],
        compiler_params=_params(("parallel",)),
        name="merge_route",
    )(h, yn, aot, u, u, wso, wo, wout, norm_ffn, wr, br)


def _deinterleave_kernel(w_ref, g_ref, l_ref, buf):
    buf[...] = w_ref[0].T
    g_ref[0] = buf[pl.ds(0, D_EXPERT, stride=2), :].astype(g_ref.dtype)
    l_ref[0] = buf[pl.ds(1, D_EXPERT, stride=2), :].astype(l_ref.dtype)


def _deinterleave(w1, tk):
    n_e = w1.shape[0]
    out = jax.ShapeDtypeStruct((n_e, D_EXPERT, D_MODEL), bf16)
    return pl.pallas_call(
        _deinterleave_kernel,
        grid=(n_e, D_MODEL // tk),
        in_specs=[pl.BlockSpec((1, tk, 2 * D_EXPERT), lambda e, j: (e, j, 0))],
        out_specs=[pl.BlockSpec((1, D_EXPERT, tk), lambda e, j: (e, 0, j)),
                   pl.BlockSpec((1, D_EXPERT, tk), lambda e, j: (e, 0, j))],
        out_shape=[out, out],
        scratch_shapes=[pltpu.VMEM((2 * D_EXPERT, tk), f32)],
        compiler_params=_params(("parallel", "parallel")),
        name="w1_deinterleave",
    )(w1)


def _row_copy_wait(src_hbm, dst_hbm, sem, n):
    pltpu.make_async_copy(src_hbm.at[pl.ds(0, n)], dst_hbm.at[pl.ds(0, n)], sem).wait()


def _dispatch_kernel(src_ref, dst_ref, x_hbm, xs_hbm, sem, *, n, n_steps):
    i = pl.program_id(0)
    slot = i % 2

    def body(r, _):
        pltpu.make_async_copy(x_hbm.at[pl.ds(src_ref[0, 0, r], 1)], xs_hbm.at[pl.ds(dst_ref[0, 0, r], 1)],
                              sem.at[slot]).start()
        return 0

    lax.fori_loop(0, n, body, 0)

    @pl.when(i > 0)
    def _():
        _row_copy_wait(x_hbm, xs_hbm, sem.at[1 - slot], n)

    @pl.when(i == n_steps - 1)
    def _():
        _row_copy_wait(x_hbm, xs_hbm, sem.at[slot], n)


def _dispatch(src, dst, xn, n_rows):
    n = DISPATCH_ROWS if n_rows % DISPATCH_ROWS == 0 else CHUNK
    n_steps = n_rows // n
    idx_spec = pl.BlockSpec((1, 1, n), lambda i: (i, 0, 0), memory_space=pltpu.SMEM)
    return pl.pallas_call(
        functools.partial(_dispatch_kernel, n=n, n_steps=n_steps),
        grid=(n_steps,),
        in_specs=[idx_spec, idx_spec, pl.BlockSpec(memory_space=pl.ANY)],
        out_specs=pl.BlockSpec(memory_space=pl.ANY),
        out_shape=jax.ShapeDtypeStruct((n_rows, D_MODEL), f32),
        scratch_shapes=[pltpu.SemaphoreType.DMA((2,))],
        compiler_params=_params(("arbitrary",)),
        name="moe_dispatch",
    )(src.reshape(n_steps, 1, n), dst.reshape(n_steps, 1, n), xn)


def _moe_kernel(be_ref, nb_ref, x_ref, w1g_ref, w1l_ref, b1g_ref, b1l_ref, w2_ref, b2_ref, o_ref):
    i = pl.program_id(0)

    @pl.when(i < nb_ref[0])
    def _():
        xb = x_ref[...].astype(bf16)
        h_glu = _dot_nt(xb, w1g_ref[0]) + b1g_ref[0]
        h_lin = _dot_nt(xb, w1l_ref[0]) + b1l_ref[0]
        h_glu = jnp.minimum(h_glu, SWIGLU_LIMIT)
        h_lin = jnp.clip(h_lin, -SWIGLU_LIMIT, SWIGLU_LIMIT)
        act = h_glu * _sigmoid(SWIGLU_ALPHA * h_glu) * (h_lin + 1.0)
        o_ref[...] = _dot(act.astype(bf16), w2_ref[0]) + b2_ref[0]

    @pl.when(i >= nb_ref[0])
    def _():
        o_ref[...] = jnp.zeros_like(o_ref)


def _moe(block_e, n_used, xs, w1g, w1l, b1g, b1l, w2, b2, bm):
    n_blocks = block_e.shape[0]
    wmap = lambda i, be, nb: (be[i], 0, 0)
    grid_spec = pltpu.PrefetchScalarGridSpec(
        num_scalar_prefetch=2,
        grid=(n_blocks,),
        in_specs=[pl.BlockSpec((bm, D_MODEL), lambda i, be, nb: (i, 0)),
                  pl.BlockSpec((1, D_EXPERT, D_MODEL), wmap),
                  pl.BlockSpec((1, D_EXPERT, D_MODEL), wmap),
                  pl.BlockSpec((1, 1, D_EXPERT), wmap),
                  pl.BlockSpec((1, 1, D_EXPERT), wmap),
                  pl.BlockSpec((1, D_EXPERT, D_MODEL), wmap),
                  pl.BlockSpec((1, 1, D_MODEL), wmap)],
        out_specs=pl.BlockSpec((bm, D_MODEL), lambda i, be, nb: (i, 0)),
    )
    return pl.pallas_call(
        _moe_kernel,
        grid_spec=grid_spec,
        out_shape=jax.ShapeDtypeStruct((n_blocks * bm, D_MODEL), f32),
        compiler_params=_params(("arbitrary",)),
        name="moe_experts",
    )(block_e, n_used, xs, w1g, w1l, b1g, b1l, w2, b2)


def _row_gather(src_hbm, dst, sem, idx_ref, n):
    def body(r, _):
        pltpu.make_async_copy(src_hbm.at[pl.ds(idx_ref[r], 1)], dst.at[pl.ds(r, 1)], sem).start()
        return 0
    lax.fori_loop(0, n, body, 0)


def _combine_kernel(pos_ref, posn_ref, rows_hbm, h_ref, tw_ref, g_ref, o_ref, gbuf, sem, *, tm, n_steps):
    i = pl.program_id(0)
    slot = i % 2

    @pl.when(i == 0)
    def _():
        _row_gather(rows_hbm, gbuf.at[0], sem.at[0], pos_ref.at[0, 0], TOP_K * tm)

    @pl.when(i + 1 < n_steps)
    def _():
        _row_gather(rows_hbm, gbuf.at[1 - slot], sem.at[1 - slot], posn_ref.at[0, 0], TOP_K * tm)

    pltpu.make_async_copy(rows_hbm.at[pl.ds(0, TOP_K * tm)], gbuf.at[slot], sem.at[slot]).wait()
    y = h_ref[...]
    tw = tw_ref[...]
    for k in range(TOP_K):
        y = y + tw[:, k:k + 1] * gbuf[slot, k * tm:(k + 1) * tm, :]
    o_ref[0] = _rms(y, g_ref[...], EPS)


def _combine(pos_tiles, moe_rows, h2, tw, norm_final, bsz, seq, lp, tm):
    tiles_per_seq = seq // tm
    seq_tiles = lp // tm
    n_steps = bsz * tiles_per_seq
    lead = (lp - seq) // tm
    rmap = lambda i: ((i // tiles_per_seq) * seq_tiles + lead + i % tiles_per_seq, 0)
    return pl.pallas_call(
        functools.partial(_combine_kernel, tm=tm, n_steps=n_steps),
        grid=(n_steps,),
        in_specs=[pl.BlockSpec((1, 1, TOP_K * tm), lambda i: (i, 0, 0), memory_space=pltpu.SMEM),
                  pl.BlockSpec((1, 1, TOP_K * tm), lambda i: (jnp.minimum(i + 1, n_steps - 1), 0, 0),
                               memory_space=pltpu.SMEM),
                  pl.BlockSpec(memory_space=pl.ANY),
                  pl.BlockSpec((tm, D_MODEL), rmap),
                  pl.BlockSpec((tm, LANES), rmap),
                  pl.BlockSpec((1, D_MODEL), lambda i: (0, 0))],
        out_specs=pl.BlockSpec((1, tm, D_MODEL), lambda i: (i // tiles_per_seq, i % tiles_per_seq, 0)),
        out_shape=jax.ShapeDtypeStruct((bsz, seq, D_MODEL), f32),
        scratch_shapes=[pltpu.VMEM((2, TOP_K * tm, D_MODEL), f32), pltpu.SemaphoreType.DMA((2,))],
        compiler_params=_params(("arbitrary",)),
        name="moe_combine",
    )(pos_tiles, pos_tiles, moe_rows, h2, tw, norm_final)


def _pack_w_in(w):
    o_dt = D_INNER + CONV_DIM
    o_qa = o_dt + SSD_HEADS
    o_kv = o_qa + Q_LORA
    o_g = o_kv + KV_LORA + QK_ROPE
    zeros = lambda n: jnp.zeros((D_MODEL, n), w.dtype)
    main = jnp.concatenate([
        w[:, :o_dt], w[:, o_g:o_g + 2 * D_MODEL],
        w[:, o_kv:o_g], zeros(LAT_QA - (KV_LORA + QK_ROPE)),
        w[:, o_qa:o_kv], zeros(LAT_W - LAT_QA - Q_LORA)], axis=1).astype(bf16)
    w_dt = jnp.concatenate([w[:, o_dt:o_qa], zeros(LANES - SSD_HEADS)], axis=1).astype(bf16)
    return main, w_dt


def _pad_lanes(v, fill=0.0):
    return jnp.concatenate([v, jnp.full((LANES - v.shape[0],), fill, v.dtype)])[None, :]


def _pack_mla(w_uq, w_ukv):
    hd = QK_NOPE + QK_ROPE
    half = QK_ROPE // 2
    wq = w_uq.reshape(Q_LORA, MLA_HEADS, hd)
    pad = jnp.zeros((Q_LORA, MLA_HEADS, LANES - hd), w_uq.dtype)
    wq_lin = jnp.concatenate([wq, pad], axis=-1)
    wq_swp = jnp.concatenate([jnp.zeros_like(wq[..., :QK_NOPE]), wq[..., QK_NOPE + half:], wq[..., QK_NOPE:QK_NOPE + half],
                              pad], axis=-1)
    wkv = w_ukv.reshape(KV_LORA, MLA_HEADS, QK_NOPE + V_HEAD)
    wk = jnp.concatenate([wkv[..., :QK_NOPE], jnp.zeros((KV_LORA, MLA_HEADS, LANES - QK_NOPE), w_ukv.dtype)], axis=-1)
    wv = jnp.concatenate([wkv[..., QK_NOPE:], jnp.zeros((KV_LORA, MLA_HEADS, LANES - V_HEAD), w_ukv.dtype)], axis=-1)
    flat = lambda a: a.reshape(a.shape[0], -1).astype(bf16)
    return flat(wq_lin), flat(wq_swp), flat(wk), flat(wv).T


def _rope_tables(lp):
    idx = jnp.arange(lp, dtype=f32)
    pos = jnp.maximum(idx - FRONT, 0.0)
    inv = ROPE_THETA ** (-jnp.arange(0, QK_ROPE, 2, dtype=f32) / QK_ROPE)
    ang = pos[:, None] * inv[None, :]
    cos, sin = jnp.cos(ang), jnp.sin(ang)
    scale = (QK_NOPE + QK_ROPE) ** -0.5 * math.log2(math.e)
    ones = jnp.ones((lp, QK_NOPE), f32)
    z_nope = jnp.zeros((lp, QK_NOPE), f32)
    z_pad = jnp.zeros((lp, LANES - QK_NOPE - QK_ROPE), f32)
    cq = jnp.concatenate([ones, cos, cos, z_pad], axis=1) * scale
    sq = jnp.concatenate([z_nope, -sin, sin, z_pad], axis=1) * scale
    ck = jnp.concatenate([z_nope, cos, cos, z_pad], axis=1)
    sk = jnp.concatenate([z_nope, -sin, sin, z_pad], axis=1)
    lane = jnp.arange(LANES)[None, :]
    qb = (lane == BIAS_LANE).astype(f32)
    kb = jnp.where((lane == BIAS_LANE) & (idx[:, None] < FRONT), NEG, 0.0).astype(f32)
    return cq, sq, ck, sk, qb, kb


def _rope_placement():
    half = QK_ROPE // 2
    r = jnp.arange(QK_ROPE)
    ek = jnp.zeros((QK_ROPE, LANES), f32).at[r, QK_NOPE + r].set(1.0)
    eks = jnp.zeros((QK_ROPE, LANES), f32).at[r, QK_NOPE + (r + half) % QK_ROPE].set(1.0)
    return ek.astype(bf16), eks.astype(bf16)


def _routing_tables(top_e, valid, bm, n_blocks):
    n_rows = n_blocks * bm
    n_tok = top_e.shape[0]
    flat_e = jnp.where(valid[:, None], top_e, N_EXPERTS).reshape(-1)
    onehot = (flat_e[:, None] == jnp.arange(N_EXPERTS, dtype=jnp.int32)[None, :]).astype(jnp.int32)
    csum = jnp.cumsum(onehot, axis=0)
    counts = csum[-1]
    rank = jnp.sum((csum - 1) * onehot, axis=1)
    padded = ((counts + bm - 1) // bm) * bm
    pends = jnp.cumsum(padded)
    pstarts = pends - padded
    dest = jnp.where(flat_e < N_EXPERTS, pstarts[jnp.minimum(flat_e, N_EXPERTS - 1)] + rank, n_rows)
    dest = dest.astype(jnp.int32).reshape(n_tok, TOP_K)
    block_e = jnp.minimum(jnp.searchsorted(pends, jnp.arange(n_blocks, dtype=jnp.int32) * bm, side="right"),
                          N_EXPERTS - 1).astype(jnp.int32)
    n_used = (pends[-1] // bm).astype(jnp.int32).reshape(1)
    seg_start = jnp.concatenate([pstarts + counts, pends[-1:]])
    seg_len = jnp.concatenate([padded - counts, n_rows - pends[-1:]])
    seg_end = jnp.cumsum(seg_len)
    return dest, block_e, n_used, seg_start, seg_end - seg_len, seg_end


def kernel(x, meta_tokens, norm_mix, w_in, conv_w, conv_b, dt_bias, a_log, d_skip, ssd_norm, w_ssd_out, q_norm, w_uq,
           kv_norm, w_ukv, w_o, w_out, norm_ffn, w_router, b_router, w_mlp1, b_mlp1, w_mlp2, b_mlp2, norm_final):
    bsz, seq, _ = x.shape
    assert seq % CHUNK == 0 and w_in.shape[0] == 1
    lp = FRONT + N_META + seq
    n_chunks = lp // CHUNK
    rows = bsz * lp
    tm = 512 if rows % 512 == 0 else CHUNK

    h = jnp.concatenate([jnp.zeros((bsz, FRONT, D_MODEL), x.dtype),
                         jnp.broadcast_to(meta_tokens[None].astype(x.dtype), (bsz, N_META, D_MODEL)), x], axis=1)
    h = h.reshape(rows, D_MODEL)

    w_main, w_dt = _pack_w_in(w_in[0])
    u = _norm_matmul(h, norm_mix, w_main, tm, U_COLS // 2, bf16, "in_proj")
    dt_raw = _norm_matmul(h, norm_mix, w_dt, tm, LANES, f32, "in_proj_dt")

    a_p = _pad_lanes(-jnp.exp(a_log[0].astype(f32)))
    head_of_col = jnp.arange(D_INNER) // SSD_HEADDIM
    e_mat = (jnp.arange(LANES)[:, None] == head_of_col[None, :]).astype(bf16)
    dsk_x = d_skip[0].astype(f32)[head_of_col][None, :]
    yn = _ssd(u, dt_raw, conv_w[0], conv_b, _pad_lanes(dt_bias[0].astype(f32)), a_p, dsk_x, ssd_norm, e_mat,
              bsz, n_chunks)

    wq, wqs, wk, wvt = _pack_mla(w_uq[0], w_ukv[0])
    ek, eks = _rope_placement()
    cq, sq, ck, sk, qb, kb = _rope_tables(lp)
    vb = (jnp.arange(MLA_HEADS * LANES) % LANES == ONES_ROW).astype(f32)[:, None]
    q, k, vt = _mla_proj(u, q_norm, kv_norm, wq, wqs, wk, wvt, ek, eks, cq, sq, ck, sk, qb, kb, vb, CHUNK, lp)
    aot = _attention(q, k, vt, bsz, lp)

    wr = jnp.concatenate([w_router[0], jnp.zeros((D_MODEL, LANES - N_EXPERTS), w_router.dtype)], axis=1).astype(bf16)
    br = _pad_lanes(b_router[0].astype(f32), NEG)
    h2, xn, te, tw = _merge(h, yn, aot, u, w_ssd_out[0].astype(bf16), w_o[0].astype(bf16), w_out[0].astype(bf16),
                            norm_ffn, wr, br, tm)

    bm = CHUNK
    n_real = bsz * (N_META + seq)
    n_blocks = -(-n_real * TOP_K // bm) + N_EXPERTS
    n_rows = n_blocks * bm
    row_id = jnp.arange(rows, dtype=jnp.int32)
    valid = (row_id % lp) >= FRONT
    dest, block_e, n_used, seg_start, seg_first, seg_end = _routing_tables(te[:, :TOP_K], valid, bm, n_blocks)
    real = lambda a: a.reshape(bsz, lp, -1)[:, FRONT:].reshape(-1)
    n_fill = n_rows - n_real * TOP_K
    fill_j = jnp.arange(n_fill, dtype=jnp.int32)
    seg = jnp.searchsorted(seg_end, fill_j, side="right")
    fill_dst = (seg_start[seg] + fill_j - seg_first[seg]).astype(jnp.int32)
    copy_src = jnp.concatenate([real(jnp.broadcast_to(row_id[:, None], (rows, TOP_K))),
                                jnp.full((n_fill,), FRONT, jnp.int32)])
    copy_dst = jnp.concatenate([real(dest), fill_dst])
    xs = _dispatch(copy_src, copy_dst, xn, n_rows)

    w1g, w1l = _deinterleave(w_mlp1[0], LANES)
    moe_rows = _moe(block_e, n_used, xs, w1g, w1l,
                    b_mlp1[0][:, None, 0::2], b_mlp1[0][:, None, 1::2],
                    w_mlp2[0].astype(bf16), b_mlp2[0][:, None, :], bm)

    tc = CHUNK
    pos = dest.reshape(bsz, lp, TOP_K)[:, lp - seq:, :]
    pos_tiles = pos.reshape(bsz * (seq // tc), tc, TOP_K).transpose(0, 2, 1).reshape(-1, 1, TOP_K * tc)
    return _combine(pos_tiles, moe_rows, h2, tw, norm_final[None, :], bsz, seq, lp, tc)
```

```python
import functools
import math

import jax
import jax.numpy as jnp
from jax import lax
from jax.experimental import pallas as pl
from jax.experimental.pallas import tpu as pltpu

f32 = jnp.float32
bf16 = jnp.bfloat16

D_MODEL = 1024
N_META = 16
D_INNER = 2048
SSD_HEADDIM = 64
SSD_HEADS = 32
SSD_GROUPS = 4
D_STATE = 128
CONV_K = 4
CONV_DIM = D_INNER + 2 * SSD_GROUPS * D_STATE
CHUNK = 256
MLA_HEADS = 16
QK_NOPE = 64
QK_ROPE = 32
V_HEAD = 64
Q_LORA = 384
KV_LORA = 256
ROPE_THETA = 10000.0
N_EXPERTS = 32
TOP_K = 4
D_EXPERT = 1024
SWIGLU_ALPHA = 1.702
SWIGLU_LIMIT = 7.0
EPS = 1e-5
MLA_EPS = 1e-6

FRONT = CHUNK - N_META
LANES = 128
SUBLANES = 8
TILE_ROWS = D_MODEL // LANES
GROUP_COLS = D_INNER // SSD_GROUPS
NEG = -1e30
TAIL = 8
BIAS_LANE = QK_NOPE + QK_ROPE
ONES_ROW = V_HEAD
ATTN_T = 3 * CHUNK
ATTN_CW = CHUNK
VT_ROWS = 80
DISPATCH_TOKENS = 512

U_Z = 0
U_X = D_INNER
U_BC = 2 * D_INNER
U_GS = U_BC + 2 * SSD_GROUPS * D_STATE
U_GM = U_GS + D_MODEL
U_LAT = U_GM + D_MODEL
LAT_W = 1024
LAT_KPE = KV_LORA
LAT_QA = 384
U_COLS = U_LAT + LAT_W

VMEM_LIMIT = 56 * 1024 * 1024


def _params(sem):
    return pltpu.CompilerParams(dimension_semantics=sem, vmem_limit_bytes=VMEM_LIMIT)


def _rms(x, g, eps):
    return x * lax.rsqrt(jnp.mean(x * x, axis=-1, keepdims=True) + eps) * g


def _sigmoid(x):
    return 1.0 / (1.0 + jnp.exp(-x))


def _dot(a, b):
    return jnp.dot(a, b, preferred_element_type=f32)


def _dot_nt(a, b):
    return lax.dot_general(a, b, (((1,), (1,)), ((), ())), preferred_element_type=f32)


def _dot_tn(a, b):
    return lax.dot_general(a, b, (((0,), (0,)), ((), ())), preferred_element_type=f32)


def _split2(v):
    hi = v.astype(bf16)
    lo = (v - hi.astype(f32)).astype(bf16)
    return hi, lo


def _split3(v):
    hi = v.astype(bf16)
    r = v - hi.astype(f32)
    mid = r.astype(bf16)
    lo = (r - mid.astype(f32)).astype(bf16)
    return hi, mid, lo


def _load_token_rows(ref, first_row, n):
    return jnp.concatenate([ref[pl.ds(first_row + s, n, stride=TILE_ROWS), :] for s in range(TILE_ROWS)], axis=1)


def _store_token_rows(ref, x):
    n = x.shape[0]
    for s in range(TILE_ROWS):
        ref[pl.ds(s, n, stride=TILE_ROWS), :] = x[:, s * LANES:(s + 1) * LANES]


def _norm_matmul_kernel(x_ref, g_ref, w_ref, o_ref):
    y = _rms(x_ref[...], g_ref[...], EPS).astype(bf16)
    o_ref[...] = _dot(y, w_ref[...]).astype(o_ref.dtype)


def _norm_matmul(x, g, w, tm, tn, out_dtype, name):
    m, k = x.shape
    n = w.shape[1]
    return pl.pallas_call(
        _norm_matmul_kernel,
        grid=(n // tn, m // tm),
        in_specs=[pl.BlockSpec((tm, k), lambda j, i: (i, 0)),
                  pl.BlockSpec((1, k), lambda j, i: (0, 0)),
                  pl.BlockSpec((k, tn), lambda j, i: (0, j))],
        out_specs=pl.BlockSpec((tm, tn), lambda j, i: (i, j)),
        out_shape=jax.ShapeDtypeStruct((m, n), out_dtype),
        compiler_params=_params(("parallel", "parallel")),
        name=name,
    )(x, g, w)


def _ssd_kernel(x_ref, bc_ref, z_ref, dt_ref, cw_ref, cb_ref, dtb_ref, a_ref, dsk_ref, nw_ref, e_ref,
                o_ref, xbuf, state, ybuf):
    c = pl.program_id(1)

    @pl.when(c == 0)
    def _():
        xbuf[0:TAIL, :] = jnp.zeros((TAIL, CONV_DIM), f32)
        state[...] = jnp.zeros_like(state)

    xbuf[TAIL:TAIL + CHUNK, 0:D_INNER] = x_ref[...].astype(f32)
    xbuf[TAIL:TAIL + CHUNK, D_INNER:CONV_DIM] = bc_ref[...].astype(f32)
    conv = cb_ref[...]
    for k in range(CONV_K):
        off = TAIL - (CONV_K - 1) + k
        conv = conv + xbuf[off:off + CHUNK, :] * cw_ref[k:k + 1, :]
    xbuf[0:TAIL, :] = xbuf[CHUNK:CHUNK + TAIL, :]
    conv = conv * _sigmoid(conv)
    xs = conv[:, 0:D_INNER]
    bm = conv[:, D_INNER:D_INNER + SSD_GROUPS * D_STATE]
    cm = conv[:, D_INNER + SSD_GROUPS * D_STATE:CONV_DIM]

    dtl = dt_ref[...] + dtb_ref[...]
    dt = jnp.maximum(dtl, 0.0) + jnp.log1p(jnp.exp(-jnp.abs(dtl)))
    row = lax.broadcasted_iota(jnp.int32, (CHUNK, LANES), 0) + c * CHUNK
    dt = jnp.where(row >= FRONT, dt, 0.0)
    da = dt * a_ref[...]

    ri = lax.broadcasted_iota(jnp.int32, (CHUNK, CHUNK), 0)
    ci = lax.broadcasted_iota(jnp.int32, (CHUNK, CHUNK), 1)
    causal = ri >= ci
    tril = causal.astype(bf16)
    acs = sum(_dot(tril, part) for part in _split3(da))
    acs_t = acs.T
    acs_last = acs[CHUNK - 1:CHUNK, :]

    e_mat = e_ref[...]

    def expand(v):
        hi, lo = _split2(v)
        return _dot(hi, e_mat) + _dot(lo, e_mat)

    dt_x = expand(dt)
    eacs_x = expand(jnp.exp(acs))
    dec_x = expand(jnp.exp(acs_last - acs))
    xdt = xs * dt_x
    xdt_b = xdt.astype(bf16)
    xdec_b = (xdt * dec_x).astype(bf16)
    bm_b = bm.astype(bf16)
    cm_b = cm.astype(bf16)
    bm_t = bm.T.astype(bf16)

    lane = lax.broadcasted_iota(jnp.int32, (CHUNK, LANES), 1)
    first_half = lane < SSD_HEADDIM
    heads_per_group = SSD_HEADS // SSD_GROUPS
    for g in range(SSD_GROUPS):
        ns = slice(g * D_STATE, (g + 1) * D_STATE)
        cb = _dot_nt(cm_b[:, ns], bm_b[:, ns])
        for jp in range(heads_per_group // 2):
            j = g * (heads_per_group // 2) + jp
            cols = slice(j * LANES, (j + 1) * LANES)
            pair = xdt_b[:, cols]
            ys = []
            for hh in range(2):
                h = 2 * j + hh
                seg = acs[:, h:h + 1] - acs_t[h:h + 1, :]
                decay = jnp.where(causal, jnp.exp(jnp.minimum(seg, 0.0)), 0.0)
                ys.append(_dot((cb * decay).astype(bf16), pair))
            ybuf[:, cols] = jnp.where(first_half, ys[0], ys[1])

    for g in range(SSD_GROUPS):
        ns = slice(g * D_STATE, (g + 1) * D_STATE)
        gc = slice(g * GROUP_COLS, (g + 1) * GROUP_COLS)
        st = state[:, gc]
        y_off = _dot(cm_b[:, ns], st.astype(bf16)) * eacs_x[:, gc]
        ybuf[:, gc] = ybuf[:, gc] + y_off
        state[:, gc] = eacs_x[CHUNK - 1:CHUNK, gc] * st + _dot(bm_t[ns, :], xdec_b[:, gc])

    zf = z_ref[...].astype(f32)
    yg = (ybuf[...] + dsk_ref[...] * xs) * (zf * _sigmoid(zf))
    for g in range(SSD_GROUPS):
        gc = slice(g * GROUP_COLS, (g + 1) * GROUP_COLS)
        blk = yg[:, gc]
        o_ref[:, gc] = (_rms(blk, nw_ref[:, gc], EPS)).astype(o_ref.dtype)


def _ssd(u, dt_raw, conv_w, conv_b, dt_bias_p, a_p, dsk_x, norm_w, e_mat, bsz, n_chunks):
    rows = bsz * n_chunks * CHUNK
    row_map = lambda b, c: b * n_chunks + c
    const = lambda b, c: (0, 0)
    return pl.pallas_call(
        _ssd_kernel,
        grid=(bsz, n_chunks),
        in_specs=[pl.BlockSpec((CHUNK, D_INNER), lambda b, c: (row_map(b, c), U_X // D_INNER)),
                  pl.BlockSpec((CHUNK, CONV_DIM - D_INNER), lambda b, c: (row_map(b, c), U_BC // (CONV_DIM - D_INNER))),
                  pl.BlockSpec((CHUNK, D_INNER), lambda b, c: (row_map(b, c), U_Z // D_INNER)),
                  pl.BlockSpec((CHUNK, LANES), lambda b, c: (row_map(b, c), 0)),
                  pl.BlockSpec((CONV_K, CONV_DIM), const),
                  pl.BlockSpec((1, CONV_DIM), const),
                  pl.BlockSpec((1, LANES), const),
                  pl.BlockSpec((1, LANES), const),
                  pl.BlockSpec((1, D_INNER), const),
                  pl.BlockSpec((1, D_INNER), const),
                  pl.BlockSpec((LANES, D_INNER), const)],
        out_specs=pl.BlockSpec((CHUNK, D_INNER), lambda b, c: (row_map(b, c), 0)),
        out_shape=jax.ShapeDtypeStruct((rows, D_INNER), bf16),
        scratch_shapes=[pltpu.VMEM((TAIL + CHUNK, CONV_DIM), f32),
                        pltpu.VMEM((D_STATE, D_INNER), f32),
                        pltpu.VMEM((CHUNK, D_INNER), f32)],
        compiler_params=_params(("parallel", "arbitrary")),
        name="ssd_scan",
    )(u, u, u, dt_raw, conv_w, conv_b, dt_bias_p, a_p, dsk_x, norm_w, e_mat)


def _mla_proj_kernel(lat_ref, qn_ref, kn_ref, wq_ref, wqs_ref, wk_ref, wvt_ref, ek_ref, eks_ref,
                     cq_ref, sq_ref, ck_ref, sk_ref, qb_ref, kb_ref, vb_ref, q_ref, k_ref, vt_ref):
    lat = lat_ref[...]
    qa = lat[:, LAT_QA:LAT_QA + Q_LORA].astype(f32)
    qn = _rms(qa, qn_ref[...], MLA_EPS).astype(bf16)
    q_lin = _dot(qn, wq_ref[...])
    q_swp = _dot(qn, wqs_ref[...])
    cn = _rms(lat[:, 0:KV_LORA].astype(f32), kn_ref[...], MLA_EPS).astype(bf16)
    k_lin = _dot(cn, wk_ref[...])
    kpe = lat[:, LAT_KPE:LAT_KPE + QK_ROPE]
    k_rot = _dot(kpe, ek_ref[...]) * ck_ref[...] + _dot(kpe, eks_ref[...]) * sk_ref[...] + kb_ref[...]
    cq = cq_ref[...]
    sq = sq_ref[...]
    qb = qb_ref[...]
    for h in range(MLA_HEADS):
        hs = slice(h * LANES, (h + 1) * LANES)
        q_ref[:, hs] = (q_lin[:, hs] * cq + q_swp[:, hs] * sq + qb).astype(q_ref.dtype)
        k_ref[:, hs] = (k_lin[:, hs] + k_rot).astype(k_ref.dtype)
    vt_ref[...] = (_dot_nt(wvt_ref[...], cn) + vb_ref[...]).astype(vt_ref.dtype)


def _mla_proj(u, q_norm, kv_norm, wq, wqs, wk, wvt, ek, eks, cq, sq, ck, sk, qb, kb, vb, tm, lp):
    rows = u.shape[0]
    tiles_per_seq = lp // tm
    const = lambda i: (0, 0)
    tab = lambda i: (i % tiles_per_seq, 0)
    hw = MLA_HEADS * LANES
    vr = MLA_HEADS * VT_ROWS
    return pl.pallas_call(
        _mla_proj_kernel,
        grid=(rows // tm,),
        in_specs=[pl.BlockSpec((tm, LAT_W), lambda i: (i, U_LAT // LAT_W)),
                  pl.BlockSpec((1, Q_LORA), const),
                  pl.BlockSpec((1, KV_LORA), const),
                  pl.BlockSpec((Q_LORA, hw), const),
                  pl.BlockSpec((Q_LORA, hw), const),
                  pl.BlockSpec((KV_LORA, hw), const),
                  pl.BlockSpec((vr, KV_LORA), const),
                  pl.BlockSpec((QK_ROPE, LANES), const),
                  pl.BlockSpec((QK_ROPE, LANES), const),
                  pl.BlockSpec((tm, LANES), tab),
                  pl.BlockSpec((tm, LANES), tab),
                  pl.BlockSpec((tm, LANES), tab),
                  pl.BlockSpec((tm, LANES), tab),
                  pl.BlockSpec((1, LANES), const),
                  pl.BlockSpec((tm, LANES), tab),
                  pl.BlockSpec((vr, 1), const)],
        out_specs=[pl.BlockSpec((tm, hw), lambda i: (i, 0)),
                   pl.BlockSpec((tm, hw), lambda i: (i, 0)),
                   pl.BlockSpec((vr, tm), lambda i: (0, i))],
        out_shape=[jax.ShapeDtypeStruct((rows, hw), bf16),
                   jax.ShapeDtypeStruct((rows, hw), bf16),
                   jax.ShapeDtypeStruct((vr, rows), bf16)],
        compiler_params=_params(("parallel",)),
        name="mla_proj",
    )(u, q_norm, kv_norm, wq, wqs, wk, wvt, ek, eks, cq, sq, ck, sk, qb, kb, vb)


def _attn_kernel(q_ref, k_ref, vt_ref, o_ref, m_st, acc_st, s_buf, p_buf, a_buf, *, t, n_qb):
    cw = ATTN_CW
    n_cg = t // cw
    chains = [(hh, cg) for hh in range(2) for cg in range(n_cg)]

    for j in range(2 * n_qb):
        m_st[j] = jnp.full((1, t), NEG, f32)
        acc_st[j] = jnp.zeros((VT_ROWS, t), f32)

    def scores(qb, kt, diag, slot):
        k0 = pl.multiple_of(kt * t, t)
        for hh, cg in chains:
            hs = slice(hh * LANES, (hh + 1) * LANES)
            cs = slice(cg * cw, (cg + 1) * cw)
            nk = (cg + 1) * cw if diag else t
            q0 = pl.multiple_of(qb * t + cg * cw, cw)
            st = _dot_nt(k_ref[pl.ds(k0, nk), hs], q_ref[pl.ds(q0, cw), hs])
            s_buf[2 * slot + hh, 0:nk, cs] = st

    def softmax(qb, kt, diag, slot):
        for hh, cg in chains:
            cs = slice(cg * cw, (cg + 1) * cw)
            nk = (cg + 1) * cw if diag else t
            st = s_buf[2 * slot + hh, 0:nk, cs]
            m_old = m_st[2 * qb + hh, :, cs]
            if diag:
                kq = lax.broadcasted_iota(jnp.int32, (nk, cw), 0) - lax.broadcasted_iota(jnp.int32, (nk, cw), 1)
                st = jnp.where(kq <= cg * cw, st, NEG)
            m_new = jnp.maximum(m_old, jnp.max(st, axis=0, keepdims=True))
            p_buf[2 * slot + hh, 0:nk, cs] = jnp.exp2((st - m_new).astype(bf16))
            a_buf[2 * slot + hh, :, cs] = jnp.exp2(m_old - m_new)
            m_st[2 * qb + hh, :, cs] = m_new

    def values(qb, kt, diag, slot):
        k0 = pl.multiple_of(kt * t, t)
        for hh, cg in chains:
            cs = slice(cg * cw, (cg + 1) * cw)
            nk = (cg + 1) * cw if diag else t
            acc_st[2 * qb + hh, :, cs] = a_buf[2 * slot + hh, :, cs] * acc_st[2 * qb + hh, :, cs] + _dot(
                vt_ref[hh * VT_ROWS:(hh + 1) * VT_ROWS, pl.ds(k0, nk)], p_buf[2 * slot + hh, 0:nk, cs])
            if diag:
                out = acc_st[2 * qb + hh, 0:V_HEAD, cs] / acc_st[2 * qb + hh, ONES_ROW:ONES_ROW + 1, cs]
                q0 = pl.multiple_of(qb * t + cg * cw, cw)
                o_ref[hh * V_HEAD:(hh + 1) * V_HEAD, pl.ds(q0, cw)] = out.astype(o_ref.dtype)

    def pipeline(n_steps, first, advance, diag):
        def step(i, cur, prev, prev2, do_scores, do_softmax, do_values):
            if do_scores:
                scores(*cur, diag, i % 2)
            if do_values:
                values(*prev2, diag, i % 2)
            if do_softmax:
                softmax(*prev, diag, (i - 1) % 2)

        if n_steps == 0:
            return
        cur, prev, prev2 = first, None, None
        i = 0
        while i < min(n_steps, 2 + (n_steps - 2) % 2):
            step(i, cur, prev, prev2, True, i >= 1, i >= 2)
            cur, prev, prev2 = advance(*cur), cur, prev
            i += 1
        if i < n_steps:
            i0 = i

            def body(_, carry):
                c, p, p2 = carry[0:2], carry[2:4], carry[4:6]
                step(i0, c, p, p2, True, True, True)
                c1 = advance(*c)
                step(i0 + 1, c1, c, p, True, True, True)
                return advance(*c1) + c1 + c

            carry = lax.fori_loop(0, (n_steps - i0) // 2, body, tuple(cur) + tuple(prev) + tuple(prev2))
            cur, prev, prev2 = carry[0:2], carry[2:4], carry[4:6]
        step(n_steps, None, prev, prev2, False, True, n_steps >= 2)
        step(n_steps + 1, None, None, prev, False, False, True)

    def next_below(qb, kt):
        wrap = kt + 1 == qb
        return qb + wrap.astype(jnp.int32), jnp.where(wrap, 0, kt + 1)

    i32 = lambda v: jnp.int32(v)
    pipeline(n_qb * (n_qb - 1) // 2, (i32(1), i32(0)), next_below, False)
    pipeline(n_qb, (i32(0), i32(0)), lambda qb, kt: (qb + 1, kt + 1), True)


def _attention(q, k, vt, bsz, lp):
    pairs = MLA_HEADS // 2
    t = ATTN_T if lp % ATTN_T == 0 else ATTN_CW
    n_qb = lp // t
    return pl.pallas_call(
        functools.partial(_attn_kernel, t=t, n_qb=n_qb),
        grid=(bsz, pairs),
        in_specs=[pl.BlockSpec((lp, 2 * LANES), lambda b, j: (b, j)),
                  pl.BlockSpec((lp, 2 * LANES), lambda b, j: (b, j)),
                  pl.BlockSpec((2 * VT_ROWS, lp), lambda b, j: (j, b))],
        out_specs=pl.BlockSpec((2 * V_HEAD, lp), lambda b, j: (j, b)),
        out_shape=jax.ShapeDtypeStruct((MLA_HEADS * V_HEAD, bsz * lp), bf16),
        scratch_shapes=[pltpu.VMEM((2 * n_qb, 1, t), f32), pltpu.VMEM((2 * n_qb, VT_ROWS, t), f32),
                        pltpu.VMEM((4, t, t), f32), pltpu.VMEM((4, t, t), bf16),
                        pltpu.VMEM((4, 1, t), f32)],
        compiler_params=_params(("parallel", "parallel")),
        name="mla_attention",
    )(q, k, vt)


def _merge_kernel(h_ref, y_ref, at_ref, gs_ref, gm_ref, wso_ref, wo_ref, wout_ref, nf_ref, wr_ref, br_ref,
                  h2_ref, xn_ref, te_ref, tw_ref):
    y_ssd = _dot(y_ref[...], wso_ref[...])
    y_mla = _dot_tn(at_ref[...], wo_ref[...])
    mix = _sigmoid(gs_ref[...].astype(f32)) * y_ssd + _sigmoid(gm_ref[...].astype(f32)) * y_mla
    h2 = h_ref[...] + _dot(mix.astype(bf16), wout_ref[...])
    h2_ref[...] = h2
    xn = _rms(h2, nf_ref[...], EPS)
    _store_token_rows(xn_ref, xn)
    logits = _dot(xn.astype(bf16), wr_ref[...]) + br_ref[...]
    lane = lax.broadcasted_iota(jnp.int32, logits.shape, 1).astype(f32)
    te = jnp.zeros(logits.shape, f32)
    tw = jnp.zeros(logits.shape, f32)
    top = None
    for k in range(TOP_K):
        m = jnp.max(logits, axis=-1, keepdims=True)
        idx = jnp.min(jnp.where(logits == m, lane, float(LANES)), axis=-1, keepdims=True)
        if top is None:
            top = m
        te = jnp.where(lane == float(k), idx, te)
        tw = jnp.where(lane == float(k), jnp.exp(m - top), tw)
        logits = jnp.where(lane == idx, -jnp.inf, logits)
    tw = tw / jnp.sum(tw, axis=-1, keepdims=True)
    te_ref[...] = te.astype(jnp.int32)
    tw_ref[...] = tw


def _merge(h, yn, aot, u, wso, wo, wout, norm_ffn, wr, br, tm):
    rows = h.shape[0]
    const = lambda i: (0, 0)
    row = lambda i: (i, 0)
    return pl.pallas_call(
        _merge_kernel,
        grid=(rows // tm,),
        in_specs=[pl.BlockSpec((tm, D_MODEL), row),
                  pl.BlockSpec((tm, D_INNER), row),
                  pl.BlockSpec((MLA_HEADS * V_HEAD, tm), lambda i: (0, i)),
                  pl.BlockSpec((tm, D_MODEL), lambda i: (i, U_GS // D_MODEL)),
                  pl.BlockSpec((tm, D_MODEL), lambda i: (i, U_GM // D_MODEL)),
                  pl.BlockSpec((D_INNER, D_MODEL), const),
                  pl.BlockSpec((D_MODEL, D_MODEL), const),
                  pl.BlockSpec((D_MODEL, D_MODEL), const),
                  pl.BlockSpec((1, D_MODEL), const),
                  pl.BlockSpec((D_MODEL, LANES), const),
                  pl.BlockSpec((1, LANES), const)],
        out_specs=[pl.BlockSpec((tm, D_MODEL), row),
                   pl.BlockSpec((tm * TILE_ROWS, LANES), row),
                   pl.BlockSpec((tm, LANES), row),
                   pl.BlockSpec((tm, LANES), row)],
        out_shape=[jax.ShapeDtypeStruct((rows, D_MODEL), f32),
                   jax.ShapeDtypeStruct((rows * TILE_ROWS, LANES), f32),
                   jax.ShapeDtypeStruct((rows, LANES), jnp.int32),
                   jax.ShapeDtypeStruct((rows, LANES), f32)],
        compiler_params=_params(("parallel",)),
        name="merge_route",
    )(h, yn, aot, u, u, wso, wo, wout, norm_ffn, wr, br)


def _deinterleave_kernel(w_ref, g_ref, l_ref, buf):
    buf[...] = w_ref[0].T
    g_ref[0] = buf[pl.ds(0, D_EXPERT, stride=2), :].astype(g_ref.dtype)
    l_ref[0] = buf[pl.ds(1, D_EXPERT, stride=2), :].astype(l_ref.dtype)


def _deinterleave(w1, tk):
    n_e = w1.shape[0]
    out = jax.ShapeDtypeStruct((n_e, D_EXPERT, D_MODEL), bf16)
    return pl.pallas_call(
        _deinterleave_kernel,
        grid=(n_e, D_MODEL // tk),
        in_specs=[pl.BlockSpec((1, tk, 2 * D_EXPERT), lambda e, j: (e, j, 0))],
        out_specs=[pl.BlockSpec((1, D_EXPERT, tk), lambda e, j: (e, 0, j)),
                   pl.BlockSpec((1, D_EXPERT, tk), lambda e, j: (e, 0, j))],
        out_shape=[out, out],
        scratch_shapes=[pltpu.VMEM((2 * D_EXPERT, tk), f32)],
        compiler_params=_params(("parallel", "parallel")),
        name="w1_deinterleave",
    )(w1)


def _tiles_wait(hbm, sem, n_tiles):
    span = hbm.at[pl.ds(0, n_tiles * TILE_ROWS)]
    pltpu.make_async_copy(span, span, sem).wait()


def _dispatch_kernel(dst_ref, fill_ref, x_ref, xs_hbm, sems, *, n_tok, n_fill, n_steps):
    i = pl.program_id(0)
    sem, fsem = sems.at[0], sems.at[1]

    def scatter(r, _):
        src = x_ref.at[pl.ds(pl.multiple_of(r * TILE_ROWS, TILE_ROWS), TILE_ROWS)]
        for k in range(TOP_K):
            row = pl.multiple_of(dst_ref[0, 0, TOP_K * r + k], TILE_ROWS)
            pltpu.make_async_copy(src, xs_hbm.at[pl.ds(row, TILE_ROWS)], sem).start()
        return 0

    lax.fori_loop(0, n_tok, scatter, 0)

    @pl.when(i == n_steps - 1)
    def _():
        def fill(j, _):
            row = pl.multiple_of(fill_ref[0, j], TILE_ROWS)
            pltpu.make_async_copy(x_ref.at[pl.ds(0, TILE_ROWS)], xs_hbm.at[pl.ds(row, TILE_ROWS)], fsem).start()
            return 0

        lax.fori_loop(0, n_fill, fill, 0)
        _tiles_wait(xs_hbm, fsem, n_fill)

    _tiles_wait(xs_hbm, sem, TOP_K * n_tok)


def _dispatch(dst, fill_dst, xn_tiles, n_tiles_out, n_tok):
    n_steps = dst.shape[0]
    n_fill = fill_dst.shape[1]
    return pl.pallas_call(
        functools.partial(_dispatch_kernel, n_tok=n_tok, n_fill=n_fill, n_steps=n_steps),
        grid=(n_steps,),
        in_specs=[pl.BlockSpec((1, 1, TOP_K * n_tok), lambda i: (i, 0, 0), memory_space=pltpu.SMEM),
                  pl.BlockSpec((1, n_fill), lambda i: (0, 0), memory_space=pltpu.SMEM),
                  pl.BlockSpec((n_tok * TILE_ROWS, LANES), lambda i: (i, 0))],
        out_specs=pl.BlockSpec(memory_space=pl.ANY),
        out_shape=jax.ShapeDtypeStruct((n_tiles_out * TILE_ROWS, LANES), f32),
        scratch_shapes=[pltpu.SemaphoreType.DMA((2,))],
        compiler_params=_params(("arbitrary",)),
        name="moe_dispatch",
    )(dst, fill_dst, xn_tiles)


def _moe_kernel(be_ref, nb_ref, x_ref, w1g_ref, w1l_ref, b1g_ref, b1l_ref, w2_ref, b2_ref, o_ref, *, bm):
    i = pl.program_id(0)

    @pl.when(i < nb_ref[0])
    def _():
        xb = _load_token_rows(x_ref, 0, bm).astype(bf16)
        h_glu = _dot_nt(xb, w1g_ref[0]) + b1g_ref[0]
        h_lin = _dot_nt(xb, w1l_ref[0]) + b1l_ref[0]
        h_glu = jnp.minimum(h_glu, SWIGLU_LIMIT)
        h_lin = jnp.clip(h_lin, -SWIGLU_LIMIT, SWIGLU_LIMIT)
        act = h_glu * _sigmoid(SWIGLU_ALPHA * h_glu) * (h_lin + 1.0)
        _store_token_rows(o_ref, _dot(act.astype(bf16), w2_ref[0]) + b2_ref[0])

    @pl.when(i >= nb_ref[0])
    def _():
        o_ref[...] = jnp.zeros_like(o_ref)


def _moe(block_e, n_used, xs, w1g, w1l, b1g, b1l, w2, b2, bm):
    n_blocks = block_e.shape[0]
    wmap = lambda i, be, nb: (be[i], 0, 0)
    grid_spec = pltpu.PrefetchScalarGridSpec(
        num_scalar_prefetch=2,
        grid=(n_blocks,),
        in_specs=[pl.BlockSpec((bm * TILE_ROWS, LANES), lambda i, be, nb: (i, 0)),
                  pl.BlockSpec((1, D_EXPERT, D_MODEL), wmap),
                  pl.BlockSpec((1, D_EXPERT, D_MODEL), wmap),
                  pl.BlockSpec((1, 1, D_EXPERT), wmap),
                  pl.BlockSpec((1, 1, D_EXPERT), wmap),
                  pl.BlockSpec((1, D_EXPERT, D_MODEL), wmap),
                  pl.BlockSpec((1, 1, D_MODEL), wmap)],
        out_specs=pl.BlockSpec((bm * TILE_ROWS, LANES), lambda i, be, nb: (i, 0)),
    )
    return pl.pallas_call(
        functools.partial(_moe_kernel, bm=bm),
        grid_spec=grid_spec,
        out_shape=jax.ShapeDtypeStruct((n_blocks * bm * TILE_ROWS, LANES), f32),
        compiler_params=_params(("arbitrary",)),
        name="moe_experts",
    )(block_e, n_used, xs, w1g, w1l, b1g, b1l, w2, b2)


def _tile_gather(src_hbm, dst, sem, idx_ref, n):
    def body(r, _):
        row = pl.multiple_of(idx_ref[r], TILE_ROWS)
        pltpu.make_async_copy(src_hbm.at[pl.ds(row, TILE_ROWS)],
                              dst.at[pl.ds(pl.multiple_of(r * TILE_ROWS, TILE_ROWS), TILE_ROWS)], sem).start()
        return 0
    lax.fori_loop(0, n, body, 0)


def _combine_kernel(pos_ref, posn_ref, rows_hbm, h_ref, tw_ref, g_ref, o_ref, gbuf, sem, *, tm, n_steps):
    i = pl.program_id(0)
    slot = i % 2
    n = TOP_K * tm

    @pl.when(i == 0)
    def _():
        _tile_gather(rows_hbm, gbuf.at[0], sem.at[0], pos_ref.at[0, 0], n)

    @pl.when(i + 1 < n_steps)
    def _():
        _tile_gather(rows_hbm, gbuf.at[1 - slot], sem.at[1 - slot], posn_ref.at[0, 0], n)

    pltpu.make_async_copy(rows_hbm.at[pl.ds(0, n * TILE_ROWS)], gbuf.at[slot], sem.at[slot]).wait()
    y = h_ref[...]
    tw = tw_ref[...]
    for k in range(TOP_K):
        y = y + tw[:, k:k + 1] * _load_token_rows(gbuf.at[slot], k * tm * TILE_ROWS, tm)
    o_ref[0] = _rms(y, g_ref[...], EPS)


def _combine(pos_tiles, moe_rows, h2, tw, norm_final, bsz, seq, lp, tm):
    tiles_per_seq = seq // tm
    seq_tiles = lp // tm
    n_steps = bsz * tiles_per_seq
    lead = (lp - seq) // tm
    rmap = lambda i: ((i // tiles_per_seq) * seq_tiles + lead + i % tiles_per_seq, 0)
    return pl.pallas_call(
        functools.partial(_combine_kernel, tm=tm, n_steps=n_steps),
        grid=(n_steps,),
        in_specs=[pl.BlockSpec((1, 1, TOP_K * tm), lambda i: (i, 0, 0), memory_space=pltpu.SMEM),
                  pl.BlockSpec((1, 1, TOP_K * tm), lambda i: (jnp.minimum(i + 1, n_steps - 1), 0, 0),
                               memory_space=pltpu.SMEM),
                  pl.BlockSpec(memory_space=pl.ANY),
                  pl.BlockSpec((tm, D_MODEL), rmap),
                  pl.BlockSpec((tm, LANES), rmap),
                  pl.BlockSpec((1, D_MODEL), lambda i: (0, 0))],
        out_specs=pl.BlockSpec((1, tm, D_MODEL), lambda i: (i // tiles_per_seq, i % tiles_per_seq, 0)),
        out_shape=jax.ShapeDtypeStruct((bsz, seq, D_MODEL), f32),
        scratch_shapes=[pltpu.VMEM((2, TOP_K * tm * TILE_ROWS, LANES), f32), pltpu.SemaphoreType.DMA((2,))],
        compiler_params=_params(("arbitrary",)),
        name="moe_combine",
    )(pos_tiles, pos_tiles, moe_rows, h2, tw, norm_final)


def _pack_w_in(w):
    o_dt = D_INNER + CONV_DIM
    o_qa = o_dt + SSD_HEADS
    o_kv = o_qa + Q_LORA
    o_g = o_kv + KV_LORA + QK_ROPE
    zeros = lambda n: jnp.zeros((D_MODEL, n), w.dtype)
    main = jnp.concatenate([
        w[:, :o_dt], w[:, o_g:o_g + 2 * D_MODEL],
        w[:, o_kv:o_g], zeros(LAT_QA - (KV_LORA + QK_ROPE)),
        w[:, o_qa:o_kv], zeros(LAT_W - LAT_QA - Q_LORA)], axis=1).astype(bf16)
    w_dt = jnp.concatenate([w[:, o_dt:o_qa], zeros(LANES - SSD_HEADS)], axis=1).astype(bf16)
    return main, w_dt


def _pad_lanes(v, fill=0.0):
    return jnp.concatenate([v, jnp.full((LANES - v.shape[0],), fill, v.dtype)])[None, :]


def _pack_mla(w_uq, w_ukv):
    hd = QK_NOPE + QK_ROPE
    half = QK_ROPE // 2
    wq = w_uq.reshape(Q_LORA, MLA_HEADS, hd)
    pad = jnp.zeros((Q_LORA, MLA_HEADS, LANES - hd), w_uq.dtype)
    wq_lin = jnp.concatenate([wq, pad], axis=-1)
    wq_swp = jnp.concatenate([jnp.zeros_like(wq[..., :QK_NOPE]), wq[..., QK_NOPE + half:], wq[..., QK_NOPE:QK_NOPE + half],
                              pad], axis=-1)
    wkv = w_ukv.reshape(KV_LORA, MLA_HEADS, QK_NOPE + V_HEAD)
    wk = jnp.concatenate([wkv[..., :QK_NOPE], jnp.zeros((KV_LORA, MLA_HEADS, LANES - QK_NOPE), w_ukv.dtype)], axis=-1)
    wv = jnp.concatenate([wkv[..., QK_NOPE:], jnp.zeros((KV_LORA, MLA_HEADS, VT_ROWS - V_HEAD), w_ukv.dtype)], axis=-1)
    flat = lambda a: a.reshape(a.shape[0], -1).astype(bf16)
    return flat(wq_lin), flat(wq_swp), flat(wk), flat(wv).T


def _rope_tables(lp):
    idx = jnp.arange(lp, dtype=f32)
    pos = jnp.maximum(idx - FRONT, 0.0)
    inv = ROPE_THETA ** (-jnp.arange(0, QK_ROPE, 2, dtype=f32) / QK_ROPE)
    ang = pos[:, None] * inv[None, :]
    cos, sin = jnp.cos(ang), jnp.sin(ang)
    scale = (QK_NOPE + QK_ROPE) ** -0.5 * math.log2(math.e)
    ones = jnp.ones((lp, QK_NOPE), f32)
    z_nope = jnp.zeros((lp, QK_NOPE), f32)
    z_pad = jnp.zeros((lp, LANES - QK_NOPE - QK_ROPE), f32)
    cq = jnp.concatenate([ones, cos, cos, z_pad], axis=1) * scale
    sq = jnp.concatenate([z_nope, -sin, sin, z_pad], axis=1) * scale
    ck = jnp.concatenate([z_nope, cos, cos, z_pad], axis=1)
    sk = jnp.concatenate([z_nope, -sin, sin, z_pad], axis=1)
    lane = jnp.arange(LANES)[None, :]
    qb = (lane == BIAS_LANE).astype(f32)
    kb = jnp.where((lane == BIAS_LANE) & (idx[:, None] < FRONT), NEG, 0.0).astype(f32)
    return cq, sq, ck, sk, qb, kb


def _rope_placement():
    half = QK_ROPE // 2
    r = jnp.arange(QK_ROPE)
    ek = jnp.zeros((QK_ROPE, LANES), f32).at[r, QK_NOPE + r].set(1.0)
    eks = jnp.zeros((QK_ROPE, LANES), f32).at[r, QK_NOPE + (r + half) % QK_ROPE].set(1.0)
    return ek.astype(bf16), eks.astype(bf16)


def _routing_tables(top_e, valid, bm, n_blocks):
    n_rows = n_blocks * bm
    n_tok = top_e.shape[0]
    flat_e = jnp.where(valid[:, None], top_e, N_EXPERTS).reshape(-1)
    onehot = (flat_e[:, None] == jnp.arange(N_EXPERTS, dtype=jnp.int32)[None, :]).astype(jnp.int32)
    csum = jnp.cumsum(onehot, axis=0)
    counts = csum[-1]
    rank = jnp.sum((csum - 1) * onehot, axis=1)
    padded = ((counts + bm - 1) // bm) * bm
    pends = jnp.cumsum(padded)
    pstarts = pends - padded
    dest = jnp.where(flat_e < N_EXPERTS, pstarts[jnp.minimum(flat_e, N_EXPERTS - 1)] + rank, n_rows)
    dest = dest.astype(jnp.int32).reshape(n_tok, TOP_K)
    block_e = jnp.minimum(jnp.searchsorted(pends, jnp.arange(n_blocks, dtype=jnp.int32) * bm, side="right"),
                          N_EXPERTS - 1).astype(jnp.int32)
    n_used = (pends[-1] // bm).astype(jnp.int32).reshape(1)
    seg_start = jnp.concatenate([pstarts + counts, pends[-1:]])
    seg_len = jnp.concatenate([padded - counts, n_rows - pends[-1:]])
    seg_end = jnp.cumsum(seg_len)
    return dest, block_e, n_used, seg_start, seg_end - seg_len, seg_end


def kernel(x, meta_tokens, norm_mix, w_in, conv_w, conv_b, dt_bias, a_log, d_skip, ssd_norm, w_ssd_out, q_norm, w_uq,
           kv_norm, w_ukv, w_o, w_out, norm_ffn, w_router, b_router, w_mlp1, b_mlp1, w_mlp2, b_mlp2, norm_final):
    bsz, seq, _ = x.shape
    assert seq % CHUNK == 0 and w_in.shape[0] == 1 and TILE_ROWS == SUBLANES
    lp = FRONT + N_META + seq
    n_chunks = lp // CHUNK
    rows = bsz * lp
    tm = 512 if rows % 512 == 0 else CHUNK

    h = jnp.concatenate([jnp.zeros((bsz, FRONT, D_MODEL), x.dtype),
                         jnp.broadcast_to(meta_tokens[None].astype(x.dtype), (bsz, N_META, D_MODEL)), x], axis=1)
    h = h.reshape(rows, D_MODEL)

    w_main, w_dt = _pack_w_in(w_in[0])
    u = _norm_matmul(h, norm_mix, w_main, tm, U_COLS // 2, bf16, "in_proj")
    dt_raw = _norm_matmul(h, norm_mix, w_dt, tm, LANES, f32, "in_proj_dt")

    a_p = _pad_lanes(-jnp.exp(a_log[0].astype(f32)))
    head_of_col = jnp.arange(D_INNER) // SSD_HEADDIM
    e_mat = (jnp.arange(LANES)[:, None] == head_of_col[None, :]).astype(bf16)
    dsk_x = d_skip[0].astype(f32)[head_of_col][None, :]
    yn = _ssd(u, dt_raw, conv_w[0], conv_b, _pad_lanes(dt_bias[0].astype(f32)), a_p, dsk_x, ssd_norm, e_mat,
              bsz, n_chunks)

    wq, wqs, wk, wvt = _pack_mla(w_uq[0], w_ukv[0])
    ek, eks = _rope_placement()
    cq, sq, ck, sk, qb, kb = _rope_tables(lp)
    vb = (jnp.arange(MLA_HEADS * VT_ROWS) % VT_ROWS == ONES_ROW).astype(f32)[:, None]
    q, k, vt = _mla_proj(u, q_norm, kv_norm, wq, wqs, wk, wvt, ek, eks, cq, sq, ck, sk, qb, kb, vb, CHUNK, lp)
    aot = _attention(q, k, vt, bsz, lp)

    wr = jnp.concatenate([w_router[0], jnp.zeros((D_MODEL, LANES - N_EXPERTS), w_router.dtype)], axis=1).astype(bf16)
    br = _pad_lanes(b_router[0].astype(f32), NEG)
    h2, xn_tiles, te, tw = _merge(h, yn, aot, u, w_ssd_out[0].astype(bf16), w_o[0].astype(bf16),
                                  w_out[0].astype(bf16), norm_ffn, wr, br, tm)

    bm = CHUNK
    n_real = bsz * (N_META + seq)
    n_blocks = -(-n_real * TOP_K // bm) + N_EXPERTS
    n_rows = n_blocks * bm
    row_id = jnp.arange(rows, dtype=jnp.int32)
    valid = (row_id % lp) >= FRONT
    dest, block_e, n_used, seg_start, seg_first, seg_end = _routing_tables(te[:, :TOP_K], valid, bm, n_blocks)
    pad_ord = (row_id // lp) * FRONT + row_id % lp
    spare = n_rows + pad_ord[:, None] * TOP_K + jnp.arange(TOP_K, dtype=jnp.int32)[None, :]
    n_tok = DISPATCH_TOKENS if rows % DISPATCH_TOKENS == 0 else CHUNK
    dst = (jnp.where(valid[:, None], dest, spare) * TILE_ROWS).reshape(rows // n_tok, 1, TOP_K * n_tok)
    n_fill = n_rows - n_real * TOP_K
    fill_j = jnp.arange(n_fill, dtype=jnp.int32)
    seg = jnp.searchsorted(seg_end, fill_j, side="right")
    fill_dst = ((seg_start[seg] + fill_j - seg_first[seg]) * TILE_ROWS).astype(jnp.int32)[None, :]
    xs = _dispatch(dst, fill_dst, xn_tiles, n_rows + bsz * FRONT * TOP_K, n_tok)

    w1g, w1l = _deinterleave(w_mlp1[0], LANES)
    moe_rows = _moe(block_e, n_used, xs, w1g, w1l,
                    b_mlp1[0][:, None, 0::2], b_mlp1[0][:, None, 1::2],
                    w_mlp2[0].astype(bf16), b_mlp2[0][:, None, :], bm)

    tc = CHUNK
    pos = dest.reshape(bsz, lp, TOP_K)[:, lp - seq:, :] * TILE_ROWS
    pos_tiles = pos.reshape(bsz * (seq // tc), tc, TOP_K).transpose(0, 2, 1).reshape(-1, 1, TOP_K * tc)
    return _combine(pos_tiles, moe_rows, h2, tw, norm_final[None, :], bsz, seq, lp, tc)
```

```python
import functools
import math

import jax
import jax.numpy as jnp
from jax import lax
from jax.experimental import pallas as pl
from jax.experimental.pallas import tpu as pltpu

f32 = jnp.float32
bf16 = jnp.bfloat16

D_MODEL = 1024
N_META = 16
D_INNER = 2048
SSD_HEADDIM = 64
SSD_HEADS = 32
SSD_GROUPS = 4
D_STATE = 128
CONV_K = 4
CONV_DIM = D_INNER + 2 * SSD_GROUPS * D_STATE
CHUNK = 256
MLA_HEADS = 16
QK_NOPE = 64
QK_ROPE = 32
V_HEAD = 64
Q_LORA = 384
KV_LORA = 256
ROPE_THETA = 10000.0
N_EXPERTS = 32
TOP_K = 4
D_EXPERT = 1024
SWIGLU_ALPHA = 1.702
SWIGLU_LIMIT = 7.0
EPS = 1e-5
MLA_EPS = 1e-6

FRONT = CHUNK - N_META
LANES = 128
SUBLANES = 8
TILE_ROWS = D_MODEL // LANES
GROUP_COLS = D_INNER // SSD_GROUPS
NEG = -1e30
TAIL = 8
BIAS_LANE = QK_NOPE + QK_ROPE
ONES_ROW = V_HEAD
ATTN_T = 3 * CHUNK
ATTN_CW = CHUNK
VT_ROWS = 80
DISPATCH_TOKENS = 512
MOE_BLOCK = 512

U_Z = 0
U_X = D_INNER
U_BC = 2 * D_INNER
U_GS = U_BC + 2 * SSD_GROUPS * D_STATE
U_GM = U_GS + D_MODEL
U_LAT = U_GM + D_MODEL
LAT_W = 1024
LAT_KPE = KV_LORA
LAT_QA = 384
U_COLS = U_LAT + LAT_W

VMEM_LIMIT = 56 * 1024 * 1024


def _params(sem):
    return pltpu.CompilerParams(dimension_semantics=sem, vmem_limit_bytes=VMEM_LIMIT)


def _rms(x, g, eps):
    return x * lax.rsqrt(jnp.mean(x * x, axis=-1, keepdims=True) + eps) * g


def _sigmoid(x):
    return 1.0 / (1.0 + jnp.exp(-x))


def _dot(a, b):
    return jnp.dot(a, b, preferred_element_type=f32)


def _dot_nt(a, b):
    return lax.dot_general(a, b, (((1,), (1,)), ((), ())), preferred_element_type=f32)


def _dot_tn(a, b):
    return lax.dot_general(a, b, (((0,), (0,)), ((), ())), preferred_element_type=f32)


def _split2(v):
    hi = v.astype(bf16)
    lo = (v - hi.astype(f32)).astype(bf16)
    return hi, lo


def _split3(v):
    hi = v.astype(bf16)
    r = v - hi.astype(f32)
    mid = r.astype(bf16)
    lo = (r - mid.astype(f32)).astype(bf16)
    return hi, mid, lo


def _load_token_rows(ref, first_row, n):
    return jnp.concatenate([ref[pl.ds(first_row + s, n, stride=TILE_ROWS), :] for s in range(TILE_ROWS)], axis=1)


def _store_token_rows(ref, x):
    n = x.shape[0]
    for s in range(TILE_ROWS):
        ref[pl.ds(s, n, stride=TILE_ROWS), :] = x[:, s * LANES:(s + 1) * LANES]


def _norm_matmul_kernel(x_ref, g_ref, w_ref, o_ref):
    y = _rms(x_ref[...], g_ref[...], EPS).astype(bf16)
    o_ref[...] = _dot(y, w_ref[...]).astype(o_ref.dtype)


def _norm_matmul(x, g, w, tm, tn, out_dtype, name):
    m, k = x.shape
    n = w.shape[1]
    return pl.pallas_call(
        _norm_matmul_kernel,
        grid=(n // tn, m // tm),
        in_specs=[pl.BlockSpec((tm, k), lambda j, i: (i, 0)),
                  pl.BlockSpec((1, k), lambda j, i: (0, 0)),
                  pl.BlockSpec((k, tn), lambda j, i: (0, j))],
        out_specs=pl.BlockSpec((tm, tn), lambda j, i: (i, j)),
        out_shape=jax.ShapeDtypeStruct((m, n), out_dtype),
        compiler_params=_params(("parallel", "parallel")),
        name=name,
    )(x, g, w)


def _ssd_kernel(x_ref, bc_ref, z_ref, dt_ref, cw_ref, cb_ref, dtb_ref, a_ref, dsk_ref, nw_ref, e_ref,
                o_ref, hbuf, state, ybuf):
    c = pl.program_id(1)

    @pl.when(c == 0)
    def _():
        hbuf[0:TAIL, :] = jnp.zeros((TAIL, CONV_DIM), f32)
        state[...] = jnp.zeros_like(state)

    ri = lax.broadcasted_iota(jnp.int32, (CHUNK, CHUNK), 0)
    ci = lax.broadcasted_iota(jnp.int32, (CHUNK, CHUNK), 1)

    x_b = jnp.concatenate([x_ref[...], bc_ref[...]], axis=1)
    x_f = x_b.astype(f32)
    conv = cb_ref[...] + x_f * cw_ref[CONV_K - 1:CONV_K, :]
    for k in range(CONV_K - 1):
        shift = (ri - ci == CONV_K - 1 - k).astype(bf16)
        conv = conv + _dot(shift, x_b) * cw_ref[k:k + 1, :]
    hbuf[TAIL:2 * TAIL, :] = x_f[0:TAIL]
    head = cb_ref[...]
    for k in range(CONV_K):
        off = TAIL - (CONV_K - 1) + k
        head = head + hbuf[off:off + TAIL, :] * cw_ref[k:k + 1, :]
    hbuf[0:TAIL, :] = x_f[CHUNK - TAIL:CHUNK]
    conv = jnp.concatenate([head, conv[TAIL:]], axis=0)
    conv = conv * _sigmoid(conv)
    xs = conv[:, 0:D_INNER]
    bm = conv[:, D_INNER:D_INNER + SSD_GROUPS * D_STATE]
    cm = conv[:, D_INNER + SSD_GROUPS * D_STATE:CONV_DIM]

    dtl = dt_ref[...] + dtb_ref[...]
    dt = jnp.maximum(dtl, 0.0) + jnp.log1p(jnp.exp(-jnp.abs(dtl)))
    row = lax.broadcasted_iota(jnp.int32, (CHUNK, LANES), 0) + c * CHUNK
    dt = jnp.where(row >= FRONT, dt, 0.0)
    da = dt * a_ref[...]

    causal = ri >= ci
    tril = causal.astype(bf16)
    acs = sum(_dot(tril, part) for part in _split3(da))
    acs2 = acs * math.log2(math.e)
    acs2_t = acs2.T
    acs_last = acs[CHUNK - 1:CHUNK, :]

    e_mat = e_ref[...]

    def expand(v):
        hi, lo = _split2(v)
        return _dot(hi, e_mat) + _dot(lo, e_mat)

    dt_x = expand(dt)
    eacs_x = expand(jnp.exp(acs))
    dec_x = expand(jnp.exp(acs_last - acs))
    xdt = xs * dt_x
    xdt_b = xdt.astype(bf16)
    xdec_b = (xdt * dec_x).astype(bf16)
    bm_b = bm.astype(bf16)
    cm_b = cm.astype(bf16)
    bm_t = bm.T.astype(bf16)

    lane = lax.broadcasted_iota(jnp.int32, (CHUNK, LANES), 1)
    first_half = lane < SSD_HEADDIM
    heads_per_group = SSD_HEADS // SSD_GROUPS
    for g in range(SSD_GROUPS):
        ns = slice(g * D_STATE, (g + 1) * D_STATE)
        cb = _dot_nt(cm_b[:, ns], bm_b[:, ns])
        for jp in range(heads_per_group // 2):
            j = g * (heads_per_group // 2) + jp
            cols = slice(j * LANES, (j + 1) * LANES)
            pair = xdt_b[:, cols]
            ys = []
            for hh in range(2):
                h = 2 * j + hh
                seg = acs2[:, h:h + 1] - acs2_t[h:h + 1, :]
                decay = jnp.where(causal, jnp.exp2(seg), 0.0)
                ys.append(_dot((cb * decay).astype(bf16), pair))
            ybuf[:, cols] = jnp.where(first_half, ys[0], ys[1])

    for g in range(SSD_GROUPS):
        ns = slice(g * D_STATE, (g + 1) * D_STATE)
        gc = slice(g * GROUP_COLS, (g + 1) * GROUP_COLS)
        st = state[:, gc]
        y_off = _dot(cm_b[:, ns], st.astype(bf16)) * eacs_x[:, gc]
        ybuf[:, gc] = ybuf[:, gc] + y_off
        state[:, gc] = eacs_x[CHUNK - 1:CHUNK, gc] * st + _dot(bm_t[ns, :], xdec_b[:, gc])

    zf = z_ref[...].astype(f32)
    yg = (ybuf[...] + dsk_ref[...] * xs) * (zf * _sigmoid(zf))
    for g in range(SSD_GROUPS):
        gc = slice(g * GROUP_COLS, (g + 1) * GROUP_COLS)
        blk = yg[:, gc]
        o_ref[:, gc] = (_rms(blk, nw_ref[:, gc], EPS)).astype(o_ref.dtype)


def _ssd(u, dt_raw, conv_w, conv_b, dt_bias_p, a_p, dsk_x, norm_w, e_mat, bsz, n_chunks):
    rows = bsz * n_chunks * CHUNK
    row_map = lambda b, c: b * n_chunks + c
    const = lambda b, c: (0, 0)
    return pl.pallas_call(
        _ssd_kernel,
        grid=(bsz, n_chunks),
        in_specs=[pl.BlockSpec((CHUNK, D_INNER), lambda b, c: (row_map(b, c), U_X // D_INNER)),
                  pl.BlockSpec((CHUNK, CONV_DIM - D_INNER), lambda b, c: (row_map(b, c), U_BC // (CONV_DIM - D_INNER))),
                  pl.BlockSpec((CHUNK, D_INNER), lambda b, c: (row_map(b, c), U_Z // D_INNER)),
                  pl.BlockSpec((CHUNK, LANES), lambda b, c: (row_map(b, c), 0)),
                  pl.BlockSpec((CONV_K, CONV_DIM), const),
                  pl.BlockSpec((1, CONV_DIM), const),
                  pl.BlockSpec((1, LANES), const),
                  pl.BlockSpec((1, LANES), const),
                  pl.BlockSpec((1, D_INNER), const),
                  pl.BlockSpec((1, D_INNER), const),
                  pl.BlockSpec((LANES, D_INNER), const)],
        out_specs=pl.BlockSpec((CHUNK, D_INNER), lambda b, c: (row_map(b, c), 0)),
        out_shape=jax.ShapeDtypeStruct((rows, D_INNER), bf16),
        scratch_shapes=[pltpu.VMEM((2 * TAIL, CONV_DIM), f32),
                        pltpu.VMEM((D_STATE, D_INNER), f32),
                        pltpu.VMEM((CHUNK, D_INNER), f32)],
        compiler_params=_params(("parallel", "arbitrary")),
        name="ssd_scan",
    )(u, u, u, dt_raw, conv_w, conv_b, dt_bias_p, a_p, dsk_x, norm_w, e_mat)


def _mla_proj_kernel(lat_ref, qn_ref, kn_ref, wq_ref, wqs_ref, wk_ref, wvt_ref, ek_ref, eks_ref,
                     cq_ref, sq_ref, ck_ref, sk_ref, qb_ref, kb_ref, vb_ref, q_ref, k_ref, vt_ref):
    lat = lat_ref[...]
    qa = lat[:, LAT_QA:LAT_QA + Q_LORA].astype(f32)
    qn = _rms(qa, qn_ref[...], MLA_EPS).astype(bf16)
    q_lin = _dot(qn, wq_ref[...])
    q_swp = _dot(qn, wqs_ref[...])
    cn = _rms(lat[:, 0:KV_LORA].astype(f32), kn_ref[...], MLA_EPS).astype(bf16)
    k_lin = _dot(cn, wk_ref[...])
    kpe = lat[:, LAT_KPE:LAT_KPE + QK_ROPE]
    k_rot = _dot(kpe, ek_ref[...]) * ck_ref[...] + _dot(kpe, eks_ref[...]) * sk_ref[...] + kb_ref[...]
    cq = cq_ref[...]
    sq = sq_ref[...]
    qb = qb_ref[...]
    for h in range(MLA_HEADS):
        hs = slice(h * LANES, (h + 1) * LANES)
        q_ref[:, hs] = (q_lin[:, hs] * cq + q_swp[:, hs] * sq + qb).astype(q_ref.dtype)
        k_ref[:, hs] = (k_lin[:, hs] + k_rot).astype(k_ref.dtype)
    vt_ref[...] = (_dot_nt(wvt_ref[...], cn) + vb_ref[...]).astype(vt_ref.dtype)


def _mla_proj(u, q_norm, kv_norm, wq, wqs, wk, wvt, ek, eks, cq, sq, ck, sk, qb, kb, vb, tm, lp):
    rows = u.shape[0]
    tiles_per_seq = lp // tm
    const = lambda i: (0, 0)
    tab = lambda i: (i % tiles_per_seq, 0)
    hw = MLA_HEADS * LANES
    vr = MLA_HEADS * VT_ROWS
    return pl.pallas_call(
        _mla_proj_kernel,
        grid=(rows // tm,),
        in_specs=[pl.BlockSpec((tm, LAT_W), lambda i: (i, U_LAT // LAT_W)),
                  pl.BlockSpec((1, Q_LORA), const),
                  pl.BlockSpec((1, KV_LORA), const),
                  pl.BlockSpec((Q_LORA, hw), const),
                  pl.BlockSpec((Q_LORA, hw), const),
                  pl.BlockSpec((KV_LORA, hw), const),
                  pl.BlockSpec((vr, KV_LORA), const),
                  pl.BlockSpec((QK_ROPE, LANES), const),
                  pl.BlockSpec((QK_ROPE, LANES), const),
                  pl.BlockSpec((tm, LANES), tab),
                  pl.BlockSpec((tm, LANES), tab),
                  pl.BlockSpec((tm, LANES), tab),
                  pl.BlockSpec((tm, LANES), tab),
                  pl.BlockSpec((1, LANES), const),
                  pl.BlockSpec((tm, LANES), tab),
                  pl.BlockSpec((vr, 1), const)],
        out_specs=[pl.BlockSpec((tm, hw), lambda i: (i, 0)),
                   pl.BlockSpec((tm, hw), lambda i: (i, 0)),
                   pl.BlockSpec((vr, tm), lambda i: (0, i))],
        out_shape=[jax.ShapeDtypeStruct((rows, hw), bf16),
                   jax.ShapeDtypeStruct((rows, hw), bf16),
                   jax.ShapeDtypeStruct((vr, rows), bf16)],
        compiler_params=_params(("parallel",)),
        name="mla_proj",
    )(u, q_norm, kv_norm, wq, wqs, wk, wvt, ek, eks, cq, sq, ck, sk, qb, kb, vb)


def _attn_kernel(q_ref, k_ref, vt_ref, o_ref, m_st, acc_st, s_buf, p_buf, a_buf, *, t, n_qb):
    cw = ATTN_CW
    n_cg = t // cw
    chains = [(hh, cg) for hh in range(2) for cg in range(n_cg)]

    for j in range(2 * n_qb):
        m_st[j] = jnp.full((1, t), NEG, f32)
        acc_st[j] = jnp.zeros((VT_ROWS, t), f32)

    def scores(qb, kt, diag, slot):
        k0 = pl.multiple_of(kt * t, t)
        for hh, cg in chains:
            hs = slice(hh * LANES, (hh + 1) * LANES)
            cs = slice(cg * cw, (cg + 1) * cw)
            nk = (cg + 1) * cw if diag else t
            q0 = pl.multiple_of(qb * t + cg * cw, cw)
            st = _dot_nt(k_ref[pl.ds(k0, nk), hs], q_ref[pl.ds(q0, cw), hs])
            s_buf[2 * slot + hh, 0:nk, cs] = st

    def softmax(qb, kt, diag, slot):
        for hh, cg in chains:
            cs = slice(cg * cw, (cg + 1) * cw)
            nk = (cg + 1) * cw if diag else t
            st = s_buf[2 * slot + hh, 0:nk, cs]
            m_old = m_st[2 * qb + hh, :, cs]
            if diag:
                kq = lax.broadcasted_iota(jnp.int32, (nk, cw), 0) - lax.broadcasted_iota(jnp.int32, (nk, cw), 1)
                st = jnp.where(kq <= cg * cw, st, NEG)
            m_new = jnp.maximum(m_old, jnp.max(st, axis=0, keepdims=True))
            p_buf[2 * slot + hh, 0:nk, cs] = jnp.exp2((st - m_new).astype(bf16))
            a_buf[2 * slot + hh, :, cs] = jnp.exp2(m_old - m_new)
            m_st[2 * qb + hh, :, cs] = m_new

    def values(qb, kt, diag, slot):
        k0 = pl.multiple_of(kt * t, t)
        for hh, cg in chains:
            cs = slice(cg * cw, (cg + 1) * cw)
            nk = (cg + 1) * cw if diag else t
            acc_st[2 * qb + hh, :, cs] = a_buf[2 * slot + hh, :, cs] * acc_st[2 * qb + hh, :, cs] + _dot(
                vt_ref[hh * VT_ROWS:(hh + 1) * VT_ROWS, pl.ds(k0, nk)], p_buf[2 * slot + hh, 0:nk, cs])
            if diag:
                out = acc_st[2 * qb + hh, 0:V_HEAD, cs] / acc_st[2 * qb + hh, ONES_ROW:ONES_ROW + 1, cs]
                q0 = pl.multiple_of(qb * t + cg * cw, cw)
                o_ref[hh * V_HEAD:(hh + 1) * V_HEAD, pl.ds(q0, cw)] = out.astype(o_ref.dtype)

    def pipeline(n_steps, first, advance, diag):
        def step(i, cur, prev, prev2, do_scores, do_softmax, do_values):
            if do_scores:
                scores(*cur, diag, i % 2)
            if do_values:
                values(*prev2, diag, i % 2)
            if do_softmax:
                softmax(*prev, diag, (i - 1) % 2)

        if n_steps == 0:
            return
        cur, prev, prev2 = first, None, None
        i = 0
        while i < min(n_steps, 2 + (n_steps - 2) % 2):
            step(i, cur, prev, prev2, True, i >= 1, i >= 2)
            cur, prev, prev2 = advance(*cur), cur, prev
            i += 1
        if i < n_steps:
            i0 = i

            def body(_, carry):
                c, p, p2 = carry[0:2], carry[2:4], carry[4:6]
                step(i0, c, p, p2, True, True, True)
                c1 = advance(*c)
                step(i0 + 1, c1, c, p, True, True, True)
                return advance(*c1) + c1 + c

            carry = lax.fori_loop(0, (n_steps - i0) // 2, body, tuple(cur) + tuple(prev) + tuple(prev2))
            cur, prev, prev2 = carry[0:2], carry[2:4], carry[4:6]
        step(n_steps, None, prev, prev2, False, True, n_steps >= 2)
        step(n_steps + 1, None, None, prev, False, False, True)

    def next_below(qb, kt):
        wrap = kt + 1 == qb
        return qb + wrap.astype(jnp.int32), jnp.where(wrap, 0, kt + 1)

    i32 = lambda v: jnp.int32(v)
    pipeline(n_qb * (n_qb - 1) // 2, (i32(1), i32(0)), next_below, False)
    pipeline(n_qb, (i32(0), i32(0)), lambda qb, kt: (qb + 1, kt + 1), True)


def _attention(q, k, vt, bsz, lp):
    pairs = MLA_HEADS // 2
    t = ATTN_T if lp % ATTN_T == 0 else ATTN_CW
    n_qb = lp // t
    return pl.pallas_call(
        functools.partial(_attn_kernel, t=t, n_qb=n_qb),
        grid=(bsz, pairs),
        in_specs=[pl.BlockSpec((lp, 2 * LANES), lambda b, j: (b, j)),
                  pl.BlockSpec((lp, 2 * LANES), lambda b, j: (b, j)),
                  pl.BlockSpec((2 * VT_ROWS, lp), lambda b, j: (j, b))],
        out_specs=pl.BlockSpec((2 * V_HEAD, lp), lambda b, j: (j, b)),
        out_shape=jax.ShapeDtypeStruct((MLA_HEADS * V_HEAD, bsz * lp), bf16),
        scratch_shapes=[pltpu.VMEM((2 * n_qb, 1, t), f32), pltpu.VMEM((2 * n_qb, VT_ROWS, t), f32),
                        pltpu.VMEM((4, t, t), f32), pltpu.VMEM((4, t, t), bf16),
                        pltpu.VMEM((4, 1, t), f32)],
        compiler_params=_params(("parallel", "parallel")),
        name="mla_attention",
    )(q, k, vt)


def _merge_kernel(h_ref, y_ref, at_ref, gs_ref, gm_ref, wso_ref, wo_ref, wout_ref, nf_ref, wr_ref, br_ref,
                  h2_ref, xn_ref, te_ref, tw_ref, tr_ref, cnt_ref, *, tm, lp):
    y_ssd = _dot(y_ref[...], wso_ref[...])
    y_mla = _dot_tn(at_ref[...], wo_ref[...])
    mix = _sigmoid(gs_ref[...].astype(f32)) * y_ssd + _sigmoid(gm_ref[...].astype(f32)) * y_mla
    h2 = h_ref[...] + _dot(mix.astype(bf16), wout_ref[...])
    h2_ref[...] = h2
    xn = _rms(h2, nf_ref[...], EPS)
    _store_token_rows(xn_ref, xn)
    logits = _dot(xn.astype(bf16), wr_ref[...]) + br_ref[...]
    lane = lax.broadcasted_iota(jnp.int32, logits.shape, 1).astype(f32)
    te = jnp.zeros(logits.shape, f32)
    tw = jnp.zeros(logits.shape, f32)
    top = None
    picks = []
    for k in range(TOP_K):
        m = jnp.max(logits, axis=-1, keepdims=True)
        idx = jnp.min(jnp.where(logits == m, lane, float(LANES)), axis=-1, keepdims=True)
        if top is None:
            top = m
        te = jnp.where(lane == float(k), idx, te)
        tw = jnp.where(lane == float(k), jnp.exp(m - top), tw)
        picks.append(lane == idx)
        logits = jnp.where(picks[-1], -jnp.inf, logits)
    tw = tw / jnp.sum(tw, axis=-1, keepdims=True)
    te_ref[...] = te.astype(jnp.int32)
    tw_ref[...] = tw

    row = (lax.broadcasted_iota(jnp.int32, (tm, LANES), 0) + pl.program_id(0) * tm).astype(f32)
    pos = row - lp * jnp.floor((row + 0.5) * (1.0 / lp))
    routed = jnp.where(pos >= FRONT, sum(p.astype(f32) for p in picks), 0.0)
    earlier = (lax.broadcasted_iota(jnp.int32, (tm, tm), 0) > lax.broadcasted_iota(jnp.int32, (tm, tm), 1))
    before = _dot(earlier.astype(bf16), routed.astype(bf16))
    tr = jnp.zeros(logits.shape, f32)
    for k in range(TOP_K):
        tr = jnp.where(lane == float(k), jnp.sum(jnp.where(picks[k], before, 0.0), axis=-1, keepdims=True), tr)
    tr_ref[...] = tr.astype(jnp.int32)
    cnt_ref[...] = jnp.broadcast_to(jnp.sum(routed, axis=0, keepdims=True), cnt_ref.shape).astype(jnp.int32)


def _merge(h, yn, aot, u, wso, wo, wout, norm_ffn, wr, br, tm, lp):
    rows = h.shape[0]
    const = lambda i: (0, 0)
    row = lambda i: (i, 0)
    return pl.pallas_call(
        functools.partial(_merge_kernel, tm=tm, lp=lp),
        grid=(rows // tm,),
        in_specs=[pl.BlockSpec((tm, D_MODEL), row),
                  pl.BlockSpec((tm, D_INNER), row),
                  pl.BlockSpec((MLA_HEADS * V_HEAD, tm), lambda i: (0, i)),
                  pl.BlockSpec((tm, D_MODEL), lambda i: (i, U_GS // D_MODEL)),
                  pl.BlockSpec((tm, D_MODEL), lambda i: (i, U_GM // D_MODEL)),
                  pl.BlockSpec((D_INNER, D_MODEL), const),
                  pl.BlockSpec((D_MODEL, D_MODEL), const),
                  pl.BlockSpec((D_MODEL, D_MODEL), const),
                  pl.BlockSpec((1, D_MODEL), const),
                  pl.BlockSpec((D_MODEL, LANES), const),
                  pl.BlockSpec((1, LANES), const)],
        out_specs=[pl.BlockSpec((tm, D_MODEL), row),
                   pl.BlockSpec((tm * TILE_ROWS, LANES), row),
                   pl.BlockSpec((tm, LANES), row),
                   pl.BlockSpec((tm, LANES), row),
                   pl.BlockSpec((tm, LANES), row),
                   pl.BlockSpec((SUBLANES, LANES), row)],
        out_shape=[jax.ShapeDtypeStruct((rows, D_MODEL), f32),
                   jax.ShapeDtypeStruct((rows * TILE_ROWS, LANES), f32),
                   jax.ShapeDtypeStruct((rows, LANES), jnp.int32),
                   jax.ShapeDtypeStruct((rows, LANES), f32),
                   jax.ShapeDtypeStruct((rows, LANES), jnp.int32),
                   jax.ShapeDtypeStruct((rows // tm * SUBLANES, LANES), jnp.int32)],
        compiler_params=_params(("parallel",)),
        name="merge_route",
    )(h, yn, aot, u, u, wso, wo, wout, norm_ffn, wr, br)


def _deinterleave_kernel(w_ref, g_ref, l_ref, buf):
    buf[...] = w_ref[0].T
    g_ref[0] = buf[pl.ds(0, D_EXPERT, stride=2), :].astype(g_ref.dtype)
    l_ref[0] = buf[pl.ds(1, D_EXPERT, stride=2), :].astype(l_ref.dtype)


def _deinterleave(w1, tk):
    n_e = w1.shape[0]
    out = jax.ShapeDtypeStruct((n_e, D_EXPERT, D_MODEL), bf16)
    return pl.pallas_call(
        _deinterleave_kernel,
        grid=(n_e, D_MODEL // tk),
        in_specs=[pl.BlockSpec((1, tk, 2 * D_EXPERT), lambda e, j: (e, j, 0))],
        out_specs=[pl.BlockSpec((1, D_EXPERT, tk), lambda e, j: (e, 0, j)),
                   pl.BlockSpec((1, D_EXPERT, tk), lambda e, j: (e, 0, j))],
        out_shape=[out, out],
        scratch_shapes=[pltpu.VMEM((2 * D_EXPERT, tk), f32)],
        compiler_params=_params(("parallel", "parallel")),
        name="w1_deinterleave",
    )(w1)


def _tiles_wait(hbm, sem, n_tiles):
    span = hbm.at[pl.ds(0, n_tiles * TILE_ROWS)]
    pltpu.make_async_copy(span, span, sem).wait()


def _dispatch_kernel(dst_ref, fill_ref, x_ref, xs_hbm, sems, *, n_tok, n_fill, n_steps):
    i = pl.program_id(0)
    sem, fsem = sems.at[0], sems.at[1]

    def scatter(r, _):
        src = x_ref.at[pl.ds(pl.multiple_of(r * TILE_ROWS, TILE_ROWS), TILE_ROWS)]
        for k in range(TOP_K):
            row = pl.multiple_of(dst_ref[0, 0, TOP_K * r + k], TILE_ROWS)
            pltpu.make_async_copy(src, xs_hbm.at[pl.ds(row, TILE_ROWS)], sem).start()
        return 0

    lax.fori_loop(0, n_tok, scatter, 0)

    @pl.when(i == n_steps - 1)
    def _():
        def fill(j, _):
            row = pl.multiple_of(fill_ref[0, j], TILE_ROWS)
            pltpu.make_async_copy(x_ref.at[pl.ds(0, TILE_ROWS)], xs_hbm.at[pl.ds(row, TILE_ROWS)], fsem).start()
            return 0

        lax.fori_loop(0, n_fill, fill, 0)
        _tiles_wait(xs_hbm, fsem, n_fill)

    _tiles_wait(xs_hbm, sem, TOP_K * n_tok)


def _dispatch(dst, fill_dst, xn_tiles, n_tiles_out, n_tok):
    n_steps = dst.shape[0]
    n_fill = fill_dst.shape[1]
    return pl.pallas_call(
        functools.partial(_dispatch_kernel, n_tok=n_tok, n_fill=n_fill, n_steps=n_steps),
        grid=(n_steps,),
        in_specs=[pl.BlockSpec((1, 1, TOP_K * n_tok), lambda i: (i, 0, 0), memory_space=pltpu.SMEM),
                  pl.BlockSpec((1, n_fill), lambda i: (0, 0), memory_space=pltpu.SMEM),
                  pl.BlockSpec((n_tok * TILE_ROWS, LANES), lambda i: (i, 0))],
        out_specs=pl.BlockSpec(memory_space=pl.ANY),
        out_shape=jax.ShapeDtypeStruct((n_tiles_out * TILE_ROWS, LANES), f32),
        scratch_shapes=[pltpu.SemaphoreType.DMA((2,))],
        compiler_params=_params(("arbitrary",)),
        name="moe_dispatch",
    )(dst, fill_dst, xn_tiles)


def _moe_kernel(be_ref, nb_ref, x_ref, w1g_ref, w1l_ref, b1g_ref, b1l_ref, w2_ref, b2_ref, o_ref, *, bm):
    i = pl.program_id(0)

    @pl.when(i < nb_ref[0])
    def _():
        xb = _load_token_rows(x_ref, 0, bm).astype(bf16)
        h_glu = _dot_nt(xb, w1g_ref[0]) + b1g_ref[0]
        h_lin = _dot_nt(xb, w1l_ref[0]) + b1l_ref[0]
        h_glu = jnp.minimum(h_glu, SWIGLU_LIMIT)
        h_lin = jnp.clip(h_lin, -SWIGLU_LIMIT, SWIGLU_LIMIT)
        act = h_glu * _sigmoid(SWIGLU_ALPHA * h_glu) * (h_lin + 1.0)
        _store_token_rows(o_ref, _dot(act.astype(bf16), w2_ref[0]) + b2_ref[0])

    @pl.when(i >= nb_ref[0])
    def _():
        o_ref[...] = jnp.zeros_like(o_ref)


def _moe(block_e, n_used, xs, w1g, w1l, b1g, b1l, w2, b2, bm):
    n_blocks = block_e.shape[0]
    wmap = lambda i, be, nb: (be[i], 0, 0)
    grid_spec = pltpu.PrefetchScalarGridSpec(
        num_scalar_prefetch=2,
        grid=(n_blocks,),
        in_specs=[pl.BlockSpec((bm * TILE_ROWS, LANES), lambda i, be, nb: (i, 0)),
                  pl.BlockSpec((1, D_EXPERT, D_MODEL), wmap),
                  pl.BlockSpec((1, D_EXPERT, D_MODEL), wmap),
                  pl.BlockSpec((1, 1, D_EXPERT), wmap),
                  pl.BlockSpec((1, 1, D_EXPERT), wmap),
                  pl.BlockSpec((1, D_EXPERT, D_MODEL), wmap),
                  pl.BlockSpec((1, 1, D_MODEL), wmap)],
        out_specs=pl.BlockSpec((bm * TILE_ROWS, LANES), lambda i, be, nb: (i, 0)),
    )
    return pl.pallas_call(
        functools.partial(_moe_kernel, bm=bm),
        grid_spec=grid_spec,
        out_shape=jax.ShapeDtypeStruct((n_blocks * bm * TILE_ROWS, LANES), f32),
        compiler_params=_params(("arbitrary",)),
        name="moe_experts",
    )(block_e, n_used, xs, w1g, w1l, b1g, b1l, w2, b2)


def _tile_gather(src_hbm, dst, sem, idx_ref, n):
    def body(r, _):
        row = pl.multiple_of(idx_ref[r], TILE_ROWS)
        pltpu.make_async_copy(src_hbm.at[pl.ds(row, TILE_ROWS)],
                              dst.at[pl.ds(pl.multiple_of(r * TILE_ROWS, TILE_ROWS), TILE_ROWS)], sem).start()
        return 0
    lax.fori_loop(0, n, body, 0)


def _combine_kernel(pos_ref, posn_ref, rows_hbm, h_ref, tw_ref, g_ref, o_ref, gbuf, sem, *, tm, n_steps):
    i = pl.program_id(0)
    slot = i % 2
    n = TOP_K * tm

    @pl.when(i == 0)
    def _():
        _tile_gather(rows_hbm, gbuf.at[0], sem.at[0], pos_ref.at[0, 0], n)

    @pl.when(i + 1 < n_steps)
    def _():
        _tile_gather(rows_hbm, gbuf.at[1 - slot], sem.at[1 - slot], posn_ref.at[0, 0], n)

    pltpu.make_async_copy(rows_hbm.at[pl.ds(0, n * TILE_ROWS)], gbuf.at[slot], sem.at[slot]).wait()
    y = h_ref[...]
    tw = tw_ref[...]
    for k in range(TOP_K):
        y = y + tw[:, k:k + 1] * _load_token_rows(gbuf.at[slot], k * tm * TILE_ROWS, tm)
    o_ref[0] = _rms(y, g_ref[...], EPS)


def _combine(pos_tiles, moe_rows, h2, tw, norm_final, bsz, seq, lp, tm):
    tiles_per_seq = seq // tm
    seq_tiles = lp // tm
    n_steps = bsz * tiles_per_seq
    lead = (lp - seq) // tm
    rmap = lambda i: ((i // tiles_per_seq) * seq_tiles + lead + i % tiles_per_seq, 0)
    return pl.pallas_call(
        functools.partial(_combine_kernel, tm=tm, n_steps=n_steps),
        grid=(n_steps,),
        in_specs=[pl.BlockSpec((1, 1, TOP_K * tm), lambda i: (i, 0, 0), memory_space=pltpu.SMEM),
                  pl.BlockSpec((1, 1, TOP_K * tm), lambda i: (jnp.minimum(i + 1, n_steps - 1), 0, 0),
                               memory_space=pltpu.SMEM),
                  pl.BlockSpec(memory_space=pl.ANY),
                  pl.BlockSpec((tm, D_MODEL), rmap),
                  pl.BlockSpec((tm, LANES), rmap),
                  pl.BlockSpec((1, D_MODEL), lambda i: (0, 0))],
        out_specs=pl.BlockSpec((1, tm, D_MODEL), lambda i: (i // tiles_per_seq, i % tiles_per_seq, 0)),
        out_shape=jax.ShapeDtypeStruct((bsz, seq, D_MODEL), f32),
        scratch_shapes=[pltpu.VMEM((2, TOP_K * tm * TILE_ROWS, LANES), f32), pltpu.SemaphoreType.DMA((2,))],
        compiler_params=_params(("arbitrary",)),
        name="moe_combine",
    )(pos_tiles, pos_tiles, moe_rows, h2, tw, norm_final)


def _pack_w_in(w):
    o_dt = D_INNER + CONV_DIM
    o_qa = o_dt + SSD_HEADS
    o_kv = o_qa + Q_LORA
    o_g = o_kv + KV_LORA + QK_ROPE
    zeros = lambda n: jnp.zeros((D_MODEL, n), w.dtype)
    main = jnp.concatenate([
        w[:, :o_dt], w[:, o_g:o_g + 2 * D_MODEL],
        w[:, o_kv:o_g], zeros(LAT_QA - (KV_LORA + QK_ROPE)),
        w[:, o_qa:o_kv], zeros(LAT_W - LAT_QA - Q_LORA)], axis=1).astype(bf16)
    w_dt = jnp.concatenate([w[:, o_dt:o_qa], zeros(LANES - SSD_HEADS)], axis=1).astype(bf16)
    return main, w_dt


def _pad_lanes(v, fill=0.0):
    return jnp.concatenate([v, jnp.full((LANES - v.shape[0],), fill, v.dtype)])[None, :]


def _pack_mla(w_uq, w_ukv):
    hd = QK_NOPE + QK_ROPE
    half = QK_ROPE // 2
    wq = w_uq.reshape(Q_LORA, MLA_HEADS, hd)
    pad = jnp.zeros((Q_LORA, MLA_HEADS, LANES - hd), w_uq.dtype)
    wq_lin = jnp.concatenate([wq, pad], axis=-1)
    wq_swp = jnp.concatenate([jnp.zeros_like(wq[..., :QK_NOPE]), wq[..., QK_NOPE + half:], wq[..., QK_NOPE:QK_NOPE + half],
                              pad], axis=-1)
    wkv = w_ukv.reshape(KV_LORA, MLA_HEADS, QK_NOPE + V_HEAD)
    wk = jnp.concatenate([wkv[..., :QK_NOPE], jnp.zeros((KV_LORA, MLA_HEADS, LANES - QK_NOPE), w_ukv.dtype)], axis=-1)
    wv = jnp.concatenate([wkv[..., QK_NOPE:], jnp.zeros((KV_LORA, MLA_HEADS, VT_ROWS - V_HEAD), w_ukv.dtype)], axis=-1)
    flat = lambda a: a.reshape(a.shape[0], -1).astype(bf16)
    return flat(wq_lin), flat(wq_swp), flat(wk), flat(wv).T


def _rope_tables(lp):
    idx = jnp.arange(lp, dtype=f32)
    pos = jnp.maximum(idx - FRONT, 0.0)
    inv = ROPE_THETA ** (-jnp.arange(0, QK_ROPE, 2, dtype=f32) / QK_ROPE)
    ang = pos[:, None] * inv[None, :]
    cos, sin = jnp.cos(ang), jnp.sin(ang)
    scale = (QK_NOPE + QK_ROPE) ** -0.5 * math.log2(math.e)
    ones = jnp.ones((lp, QK_NOPE), f32)
    z_nope = jnp.zeros((lp, QK_NOPE), f32)
    z_pad = jnp.zeros((lp, LANES - QK_NOPE - QK_ROPE), f32)
    cq = jnp.concatenate([ones, cos, cos, z_pad], axis=1) * scale
    sq = jnp.concatenate([z_nope, -sin, sin, z_pad], axis=1) * scale
    ck = jnp.concatenate([z_nope, cos, cos, z_pad], axis=1)
    sk = jnp.concatenate([z_nope, -sin, sin, z_pad], axis=1)
    lane = jnp.arange(LANES)[None, :]
    qb = (lane == BIAS_LANE).astype(f32)
    kb = jnp.where((lane == BIAS_LANE) & (idx[:, None] < FRONT), NEG, 0.0).astype(f32)
    return cq, sq, ck, sk, qb, kb


def _rope_placement():
    half = QK_ROPE // 2
    r = jnp.arange(QK_ROPE)
    ek = jnp.zeros((QK_ROPE, LANES), f32).at[r, QK_NOPE + r].set(1.0)
    eks = jnp.zeros((QK_ROPE, LANES), f32).at[r, QK_NOPE + (r + half) % QK_ROPE].set(1.0)
    return ek.astype(bf16), eks.astype(bf16)


def _routing_tables(top_e, tile_rank, tile_counts, tm, bm, n_blocks):
    n_rows = n_blocks * bm
    counts = jnp.sum(tile_counts, axis=0)
    padded = ((counts + bm - 1) // bm) * bm
    pends = jnp.cumsum(padded)
    pstarts = pends - padded
    tile_start = pstarts[None, :] + jnp.cumsum(tile_counts, axis=0) - tile_counts
    row_start = jnp.repeat(tile_start, tm, axis=0)
    pick = top_e[:, :, None] == jnp.arange(N_EXPERTS, dtype=jnp.int32)[None, None, :]
    dest = (jnp.sum(jnp.where(pick, row_start[:, None, :], 0), axis=-1) + tile_rank).astype(jnp.int32)
    block_e = jnp.minimum(jnp.searchsorted(pends, jnp.arange(n_blocks, dtype=jnp.int32) * bm, side="right"),
                          N_EXPERTS - 1).astype(jnp.int32)
    n_used = (pends[-1] // bm).astype(jnp.int32).reshape(1)
    seg_start = jnp.concatenate([pstarts + counts, pends[-1:]])
    seg_len = jnp.concatenate([padded - counts, n_rows - pends[-1:]])
    seg_end = jnp.cumsum(seg_len)
    return dest, block_e, n_used, seg_start, seg_end - seg_len, seg_end


def kernel(x, meta_tokens, norm_mix, w_in, conv_w, conv_b, dt_bias, a_log, d_skip, ssd_norm, w_ssd_out, q_norm, w_uq,
           kv_norm, w_ukv, w_o, w_out, norm_ffn, w_router, b_router, w_mlp1, b_mlp1, w_mlp2, b_mlp2, norm_final):
    bsz, seq, _ = x.shape
    assert seq % CHUNK == 0 and w_in.shape[0] == 1 and TILE_ROWS == SUBLANES
    lp = FRONT + N_META + seq
    n_chunks = lp // CHUNK
    rows = bsz * lp
    tm = 512 if rows % 512 == 0 else CHUNK

    h = jnp.concatenate([jnp.zeros((bsz, FRONT, D_MODEL), x.dtype),
                         jnp.broadcast_to(meta_tokens[None].astype(x.dtype), (bsz, N_META, D_MODEL)), x], axis=1)
    h = h.reshape(rows, D_MODEL)

    w_main, w_dt = _pack_w_in(w_in[0])
    u = _norm_matmul(h, norm_mix, w_main, tm, U_COLS // 2, bf16, "in_proj")
    dt_raw = _norm_matmul(h, norm_mix, w_dt, tm, LANES, f32, "in_proj_dt")

    a_p = _pad_lanes(-jnp.exp(a_log[0].astype(f32)))
    head_of_col = jnp.arange(D_INNER) // SSD_HEADDIM
    e_mat = (jnp.arange(LANES)[:, None] == head_of_col[None, :]).astype(bf16)
    dsk_x = d_skip[0].astype(f32)[head_of_col][None, :]
    yn = _ssd(u, dt_raw, conv_w[0], conv_b, _pad_lanes(dt_bias[0].astype(f32)), a_p, dsk_x, ssd_norm, e_mat,
              bsz, n_chunks)

    wq, wqs, wk, wvt = _pack_mla(w_uq[0], w_ukv[0])
    ek, eks = _rope_placement()
    cq, sq, ck, sk, qb, kb = _rope_tables(lp)
    vb = (jnp.arange(MLA_HEADS * VT_ROWS) % VT_ROWS == ONES_ROW).astype(f32)[:, None]
    q, k, vt = _mla_proj(u, q_norm, kv_norm, wq, wqs, wk, wvt, ek, eks, cq, sq, ck, sk, qb, kb, vb, CHUNK, lp)
    aot = _attention(q, k, vt, bsz, lp)

    wr = jnp.concatenate([w_router[0], jnp.zeros((D_MODEL, LANES - N_EXPERTS), w_router.dtype)], axis=1).astype(bf16)
    br = _pad_lanes(b_router[0].astype(f32), NEG)
    h2, xn_tiles, te, tw, tr, cnt = _merge(h, yn, aot, u, w_ssd_out[0].astype(bf16), w_o[0].astype(bf16),
                                           w_out[0].astype(bf16), norm_ffn, wr, br, tm, lp)

    bm = MOE_BLOCK
    n_real = bsz * (N_META + seq)
    n_blocks = -(-n_real * TOP_K // bm) + N_EXPERTS
    n_rows = n_blocks * bm
    row_id = jnp.arange(rows, dtype=jnp.int32)
    valid = (row_id % lp) >= FRONT
    tile_counts = cnt.reshape(rows // tm, SUBLANES, LANES)[:, 0, :N_EXPERTS]
    dest, block_e, n_used, seg_start, seg_first, seg_end = _routing_tables(
        te[:, :TOP_K], tr[:, :TOP_K], tile_counts, tm, bm, n_blocks)
    pad_ord = (row_id // lp) * FRONT + row_id % lp
    spare = n_rows + pad_ord[:, None] * TOP_K + jnp.arange(TOP_K, dtype=jnp.int32)[None, :]
    n_tok = DISPATCH_TOKENS if rows % DISPATCH_TOKENS == 0 else CHUNK
    dst = (jnp.where(valid[:, None], dest, spare) * TILE_ROWS).reshape(rows // n_tok, 1, TOP_K * n_tok)
    n_fill = n_rows - n_real * TOP_K
    fill_j = jnp.arange(n_fill, dtype=jnp.int32)
    seg = jnp.searchsorted(seg_end, fill_j, side="right")
    fill_dst = ((seg_start[seg] + fill_j - seg_first[seg]) * TILE_ROWS).astype(jnp.int32)[None, :]
    xs = _dispatch(dst, fill_dst, xn_tiles, n_rows + bsz * FRONT * TOP_K, n_tok)

    w1g, w1l = _deinterleave(w_mlp1[0], LANES)
    moe_rows = _moe(block_e, n_used, xs, w1g, w1l,
                    b_mlp1[0][:, None, 0::2], b_mlp1[0][:, None, 1::2],
                    w_mlp2[0].astype(bf16), b_mlp2[0][:, None, :], bm)

    tc = CHUNK
    pos = dest.reshape(bsz, lp, TOP_K)[:, lp - seq:, :] * TILE_ROWS
    pos_tiles = pos.reshape(bsz * (seq // tc), tc, TOP_K).transpose(0, 2, 1).reshape(-1, 1, TOP_K * tc)
    return _combine(pos_tiles, moe_rows, h2, tw, norm_final[None, :], bsz, seq, lp, tc)
```

```python
import functools
import math

import jax
import jax.numpy as jnp
from jax import lax
from jax.experimental import pallas as pl
from jax.experimental.pallas import tpu as pltpu

f32 = jnp.float32
bf16 = jnp.bfloat16

D_MODEL = 1024
N_META = 16
D_INNER = 2048
SSD_HEADDIM = 64
SSD_HEADS = 32
SSD_GROUPS = 4
D_STATE = 128
CONV_K = 4
CONV_DIM = D_INNER + 2 * SSD_GROUPS * D_STATE
CHUNK = 256
MLA_HEADS = 16
QK_NOPE = 64
QK_ROPE = 32
V_HEAD = 64
Q_LORA = 384
KV_LORA = 256
ROPE_THETA = 10000.0
N_EXPERTS = 32
TOP_K = 4
D_EXPERT = 1024
SWIGLU_ALPHA = 1.702
SWIGLU_LIMIT = 7.0
EPS = 1e-5
MLA_EPS = 1e-6

FRONT = CHUNK - N_META
LANES = 128
SUBLANES = 8
TILE_ROWS = D_MODEL // LANES
GROUP_COLS = D_INNER // SSD_GROUPS
NEG = -1e30
TAIL = 8
BIAS_LANE = QK_NOPE + QK_ROPE
ONES_ROW = V_HEAD
ATTN_T = 3 * CHUNK
ATTN_CW = CHUNK
VT_ROWS = 80
DISPATCH_TOKENS = 512
MOE_BLOCK = 512

U_Z = 0
U_X = D_INNER
U_BC = 2 * D_INNER
U_GS = U_BC + 2 * SSD_GROUPS * D_STATE
U_GM = U_GS + D_MODEL
U_LAT = U_GM + D_MODEL
LAT_W = 1024
LAT_KPE = KV_LORA
LAT_QA = 384
U_COLS = U_LAT + LAT_W

VMEM_LIMIT = 56 * 1024 * 1024


def _params(sem):
    return pltpu.CompilerParams(dimension_semantics=sem, vmem_limit_bytes=VMEM_LIMIT)


def _rms(x, g, eps):
    return x * lax.rsqrt(jnp.mean(x * x, axis=-1, keepdims=True) + eps) * g


def _sigmoid(x):
    return 1.0 / (1.0 + jnp.exp(-x))


def _dot(a, b):
    return jnp.dot(a, b, preferred_element_type=f32)


def _dot_nt(a, b):
    return lax.dot_general(a, b, (((1,), (1,)), ((), ())), preferred_element_type=f32)


def _dot_tn(a, b):
    return lax.dot_general(a, b, (((0,), (0,)), ((), ())), preferred_element_type=f32)


def _split2(v):
    hi = v.astype(bf16)
    lo = (v - hi.astype(f32)).astype(bf16)
    return hi, lo


def _split3(v):
    hi = v.astype(bf16)
    r = v - hi.astype(f32)
    mid = r.astype(bf16)
    lo = (r - mid.astype(f32)).astype(bf16)
    return hi, mid, lo


def _load_token_rows(ref, first_row, n):
    return jnp.concatenate([ref[pl.ds(first_row + s, n, stride=TILE_ROWS), :] for s in range(TILE_ROWS)], axis=1)


def _store_token_rows(ref, x):
    n = x.shape[0]
    for s in range(TILE_ROWS):
        ref[pl.ds(s, n, stride=TILE_ROWS), :] = x[:, s * LANES:(s + 1) * LANES]


def _norm_matmul_kernel(x_ref, g_ref, w_ref, o_ref):
    y = _rms(x_ref[...], g_ref[...], EPS).astype(bf16)
    o_ref[...] = _dot(y, w_ref[...]).astype(o_ref.dtype)


def _norm_matmul(x, g, w, tm, tn, out_dtype, name):
    m, k = x.shape
    n = w.shape[1]
    return pl.pallas_call(
        _norm_matmul_kernel,
        grid=(n // tn, m // tm),
        in_specs=[pl.BlockSpec((tm, k), lambda j, i: (i, 0)),
                  pl.BlockSpec((1, k), lambda j, i: (0, 0)),
                  pl.BlockSpec((k, tn), lambda j, i: (0, j))],
        out_specs=pl.BlockSpec((tm, tn), lambda j, i: (i, j)),
        out_shape=jax.ShapeDtypeStruct((m, n), out_dtype),
        compiler_params=_params(("parallel", "parallel")),
        name=name,
    )(x, g, w)


def _ssd_kernel(x_ref, bc_ref, z_ref, dt_ref, cw_ref, cb_ref, dtb_ref, a_ref, dsk_ref, nw_ref, e_ref,
                o_ref, hbuf, state, ybuf):
    c = pl.program_id(1)

    @pl.when(c == 0)
    def _():
        hbuf[0:TAIL, :] = jnp.zeros((TAIL, CONV_DIM), f32)
        state[...] = jnp.zeros_like(state)

    ri = lax.broadcasted_iota(jnp.int32, (CHUNK, CHUNK), 0)
    ci = lax.broadcasted_iota(jnp.int32, (CHUNK, CHUNK), 1)

    x_b = jnp.concatenate([x_ref[...], bc_ref[...]], axis=1)
    x_f = x_b.astype(f32)
    conv = cb_ref[...] + x_f * cw_ref[CONV_K - 1:CONV_K, :]
    for k in range(CONV_K - 1):
        shift = (ri - ci == CONV_K - 1 - k).astype(bf16)
        conv = conv + _dot(shift, x_b) * cw_ref[k:k + 1, :]
    hbuf[TAIL:2 * TAIL, :] = x_f[0:TAIL]
    head = cb_ref[...]
    for k in range(CONV_K):
        off = TAIL - (CONV_K - 1) + k
        head = head + hbuf[off:off + TAIL, :] * cw_ref[k:k + 1, :]
    hbuf[0:TAIL, :] = x_f[CHUNK - TAIL:CHUNK]
    conv = jnp.concatenate([head, conv[TAIL:]], axis=0)
    conv = conv * _sigmoid(conv)
    xs = conv[:, 0:D_INNER]
    bm = conv[:, D_INNER:D_INNER + SSD_GROUPS * D_STATE]
    cm = conv[:, D_INNER + SSD_GROUPS * D_STATE:CONV_DIM]

    dtl = dt_ref[...] + dtb_ref[...]
    dt = jnp.maximum(dtl, 0.0) + jnp.log1p(jnp.exp(-jnp.abs(dtl)))
    row = lax.broadcasted_iota(jnp.int32, (CHUNK, LANES), 0) + c * CHUNK
    dt = jnp.where(row >= FRONT, dt, 0.0)
    da = dt * a_ref[...]

    causal = ri >= ci
    tril = causal.astype(bf16)
    acs = sum(_dot(tril, part) for part in _split3(da))
    acs2 = acs * math.log2(math.e)
    acs2_t = acs2.T
    acs_last = acs[CHUNK - 1:CHUNK, :]

    e_mat = e_ref[...]

    def expand(v):
        hi, lo = _split2(v)
        return _dot(hi, e_mat) + _dot(lo, e_mat)

    dt_x = expand(dt)
    eacs_x = expand(jnp.exp(acs))
    dec_x = expand(jnp.exp(acs_last - acs))
    xdt = xs * dt_x
    xdt_b = xdt.astype(bf16)
    xdec_b = (xdt * dec_x).astype(bf16)
    bm_b = bm.astype(bf16)
    cm_b = cm.astype(bf16)
    bm_t = bm.T.astype(bf16)

    lane = lax.broadcasted_iota(jnp.int32, (CHUNK, LANES), 1)
    first_half = lane < SSD_HEADDIM
    heads_per_group = SSD_HEADS // SSD_GROUPS
    for g in range(SSD_GROUPS):
        ns = slice(g * D_STATE, (g + 1) * D_STATE)
        cb = _dot_nt(cm_b[:, ns], bm_b[:, ns])
        for jp in range(heads_per_group // 2):
            j = g * (heads_per_group // 2) + jp
            cols = slice(j * LANES, (j + 1) * LANES)
            pair = xdt_b[:, cols]
            ys = []
            for hh in range(2):
                h = 2 * j + hh
                seg = acs2[:, h:h + 1] - acs2_t[h:h + 1, :]
                decay = jnp.where(causal, jnp.exp2(seg), 0.0)
                ys.append(_dot((cb * decay).astype(bf16), pair))
            ybuf[:, cols] = jnp.where(first_half, ys[0], ys[1])

    for g in range(SSD_GROUPS):
        ns = slice(g * D_STATE, (g + 1) * D_STATE)
        gc = slice(g * GROUP_COLS, (g + 1) * GROUP_COLS)
        st = state[:, gc]
        y_off = _dot(cm_b[:, ns], st.astype(bf16)) * eacs_x[:, gc]
        ybuf[:, gc] = ybuf[:, gc] + y_off
        state[:, gc] = eacs_x[CHUNK - 1:CHUNK, gc] * st + _dot(bm_t[ns, :], xdec_b[:, gc])

    zf = z_ref[...].astype(f32)
    yg = (ybuf[...] + dsk_ref[...] * xs) * (zf * _sigmoid(zf))
    for g in range(SSD_GROUPS):
        gc = slice(g * GROUP_COLS, (g + 1) * GROUP_COLS)
        blk = yg[:, gc]
        o_ref[:, gc] = (_rms(blk, nw_ref[:, gc], EPS)).astype(o_ref.dtype)


def _ssd(u, dt_raw, conv_w, conv_b, dt_bias_p, a_p, dsk_x, norm_w, e_mat, bsz, n_chunks):
    rows = bsz * n_chunks * CHUNK
    row_map = lambda b, c: b * n_chunks + c
    const = lambda b, c: (0, 0)
    return pl.pallas_call(
        _ssd_kernel,
        grid=(bsz, n_chunks),
        in_specs=[pl.BlockSpec((CHUNK, D_INNER), lambda b, c: (row_map(b, c), U_X // D_INNER)),
                  pl.BlockSpec((CHUNK, CONV_DIM - D_INNER), lambda b, c: (row_map(b, c), U_BC // (CONV_DIM - D_INNER))),
                  pl.BlockSpec((CHUNK, D_INNER), lambda b, c: (row_map(b, c), U_Z // D_INNER)),
                  pl.BlockSpec((CHUNK, LANES), lambda b, c: (row_map(b, c), 0)),
                  pl.BlockSpec((CONV_K, CONV_DIM), const),
                  pl.BlockSpec((1, CONV_DIM), const),
                  pl.BlockSpec((1, LANES), const),
                  pl.BlockSpec((1, LANES), const),
                  pl.BlockSpec((1, D_INNER), const),
                  pl.BlockSpec((1, D_INNER), const),
                  pl.BlockSpec((LANES, D_INNER), const)],
        out_specs=pl.BlockSpec((CHUNK, D_INNER), lambda b, c: (row_map(b, c), 0)),
        out_shape=jax.ShapeDtypeStruct((rows, D_INNER), bf16),
        scratch_shapes=[pltpu.VMEM((2 * TAIL, CONV_DIM), f32),
                        pltpu.VMEM((D_STATE, D_INNER), f32),
                        pltpu.VMEM((CHUNK, D_INNER), f32)],
        compiler_params=_params(("parallel", "arbitrary")),
        name="ssd_scan",
    )(u, u, u, dt_raw, conv_w, conv_b, dt_bias_p, a_p, dsk_x, norm_w, e_mat)


def _mla_proj_kernel(lat_ref, qn_ref, kn_ref, wq_ref, wqs_ref, wk_ref, wvt_ref, ek_ref, eks_ref,
                     cq_ref, sq_ref, ck_ref, sk_ref, qb_ref, kb_ref, vb_ref, q_ref, k_ref, vt_ref):
    lat = lat_ref[...]
    qa = lat[:, LAT_QA:LAT_QA + Q_LORA].astype(f32)
    qn = _rms(qa, qn_ref[...], MLA_EPS).astype(bf16)
    q_lin = _dot(qn, wq_ref[...])
    q_swp = _dot(qn, wqs_ref[...])
    cn = _rms(lat[:, 0:KV_LORA].astype(f32), kn_ref[...], MLA_EPS).astype(bf16)
    k_lin = _dot(cn, wk_ref[...])
    kpe = lat[:, LAT_KPE:LAT_KPE + QK_ROPE]
    k_rot = _dot(kpe, ek_ref[...]) * ck_ref[...] + _dot(kpe, eks_ref[...]) * sk_ref[...] + kb_ref[...]
    cq = cq_ref[...]
    sq = sq_ref[...]
    qb = qb_ref[...]
    for h in range(MLA_HEADS):
        hs = slice(h * LANES, (h + 1) * LANES)
        q_ref[:, hs] = (q_lin[:, hs] * cq + q_swp[:, hs] * sq + qb).astype(q_ref.dtype)
        k_ref[:, hs] = (k_lin[:, hs] + k_rot).astype(k_ref.dtype)
    vt_ref[...] = (_dot_nt(wvt_ref[...], cn) + vb_ref[...]).astype(vt_ref.dtype)


def _mla_proj(u, q_norm, kv_norm, wq, wqs, wk, wvt, ek, eks, cq, sq, ck, sk, qb, kb, vb, tm, lp):
    rows = u.shape[0]
    tiles_per_seq = lp // tm
    const = lambda i: (0, 0)
    tab = lambda i: (i % tiles_per_seq, 0)
    hw = MLA_HEADS * LANES
    vr = MLA_HEADS * VT_ROWS
    return pl.pallas_call(
        _mla_proj_kernel,
        grid=(rows // tm,),
        in_specs=[pl.BlockSpec((tm, LAT_W), lambda i: (i, U_LAT // LAT_W)),
                  pl.BlockSpec((1, Q_LORA), const),
                  pl.BlockSpec((1, KV_LORA), const),
                  pl.BlockSpec((Q_LORA, hw), const),
                  pl.BlockSpec((Q_LORA, hw), const),
                  pl.BlockSpec((KV_LORA, hw), const),
                  pl.BlockSpec((vr, KV_LORA), const),
                  pl.BlockSpec((QK_ROPE, LANES), const),
                  pl.BlockSpec((QK_ROPE, LANES), const),
                  pl.BlockSpec((tm, LANES), tab),
                  pl.BlockSpec((tm, LANES), tab),
                  pl.BlockSpec((tm, LANES), tab),
                  pl.BlockSpec((tm, LANES), tab),
                  pl.BlockSpec((1, LANES), const),
                  pl.BlockSpec((tm, LANES), tab),
                  pl.BlockSpec((vr, 1), const)],
        out_specs=[pl.BlockSpec((tm, hw), lambda i: (i, 0)),
                   pl.BlockSpec((tm, hw), lambda i: (i, 0)),
                   pl.BlockSpec((vr, tm), lambda i: (0, i))],
        out_shape=[jax.ShapeDtypeStruct((rows, hw), bf16),
                   jax.ShapeDtypeStruct((rows, hw), bf16),
                   jax.ShapeDtypeStruct((vr, rows), bf16)],
        compiler_params=_params(("parallel",)),
        name="mla_proj",
    )(u, q_norm, kv_norm, wq, wqs, wk, wvt, ek, eks, cq, sq, ck, sk, qb, kb, vb)


def _attn_kernel(q_ref, k_ref, vt_ref, o_ref, m_st, acc_st, s_buf, p_buf, a_buf, *, t, n_qb):
    cw = ATTN_CW
    n_cg = t // cw
    chains = [(hh, cg) for hh in range(2) for cg in range(n_cg)]

    for j in range(2 * n_qb):
        m_st[j] = jnp.full((1, t), NEG, f32)
        acc_st[j] = jnp.zeros((VT_ROWS, t), f32)

    def scores(qb, kt, diag, slot):
        k0 = pl.multiple_of(kt * t, t)
        for hh, cg in chains:
            hs = slice(hh * LANES, (hh + 1) * LANES)
            cs = slice(cg * cw, (cg + 1) * cw)
            nk = (cg + 1) * cw if diag else t
            q0 = pl.multiple_of(qb * t + cg * cw, cw)
            st = _dot_nt(k_ref[pl.ds(k0, nk), hs], q_ref[pl.ds(q0, cw), hs])
            s_buf[2 * slot + hh, 0:nk, cs] = st

    def softmax(qb, kt, diag, slot):
        for hh, cg in chains:
            cs = slice(cg * cw, (cg + 1) * cw)
            nk = (cg + 1) * cw if diag else t
            st = s_buf[2 * slot + hh, 0:nk, cs]
            m_old = m_st[2 * qb + hh, :, cs]
            if diag:
                kq = lax.broadcasted_iota(jnp.int32, (nk, cw), 0) - lax.broadcasted_iota(jnp.int32, (nk, cw), 1)
                st = jnp.where(kq <= cg * cw, st, NEG)
            m_new = jnp.maximum(m_old, jnp.max(st, axis=0, keepdims=True))
            p_buf[2 * slot + hh, 0:nk, cs] = jnp.exp2((st - m_new).astype(bf16))
            a_buf[2 * slot + hh, :, cs] = jnp.exp2(m_old - m_new)
            m_st[2 * qb + hh, :, cs] = m_new

    def values(qb, kt, diag, slot):
        k0 = pl.multiple_of(kt * t, t)
        for hh, cg in chains:
            cs = slice(cg * cw, (cg + 1) * cw)
            nk = (cg + 1) * cw if diag else t
            acc_st[2 * qb + hh, :, cs] = a_buf[2 * slot + hh, :, cs] * acc_st[2 * qb + hh, :, cs] + _dot(
                vt_ref[hh * VT_ROWS:(hh + 1) * VT_ROWS, pl.ds(k0, nk)], p_buf[2 * slot + hh, 0:nk, cs])
            if diag:
                out = acc_st[2 * qb + hh, 0:V_HEAD, cs] / acc_st[2 * qb + hh, ONES_ROW:ONES_ROW + 1, cs]
                q0 = pl.multiple_of(qb * t + cg * cw, cw)
                o_ref[hh * V_HEAD:(hh + 1) * V_HEAD, pl.ds(q0, cw)] = out.astype(o_ref.dtype)

    def pipeline(n_steps, first, advance, diag):
        def step(i, cur, prev, prev2, do_scores, do_softmax, do_values):
            if do_scores:
                scores(*cur, diag, i % 2)
            if do_values:
                values(*prev2, diag, i % 2)
            if do_softmax:
                softmax(*prev, diag, (i - 1) % 2)

        if n_steps == 0:
            return
        cur, prev, prev2 = first, None, None
        i = 0
        while i < min(n_steps, 2 + (n_steps - 2) % 2):
            step(i, cur, prev, prev2, True, i >= 1, i >= 2)
            cur, prev, prev2 = advance(*cur), cur, prev
            i += 1
        if i < n_steps:
            i0 = i

            def body(_, carry):
                c, p, p2 = carry[0:2], carry[2:4], carry[4:6]
                step(i0, c, p, p2, True, True, True)
                c1 = advance(*c)
                step(i0 + 1, c1, c, p, True, True, True)
                return advance(*c1) + c1 + c

            carry = lax.fori_loop(0, (n_steps - i0) // 2, body, tuple(cur) + tuple(prev) + tuple(prev2))
            cur, prev, prev2 = carry[0:2], carry[2:4], carry[4:6]
        step(n_steps, None, prev, prev2, False, True, n_steps >= 2)
        step(n_steps + 1, None, None, prev, False, False, True)

    def next_below(qb, kt):
        wrap = kt + 1 == qb
        return qb + wrap.astype(jnp.int32), jnp.where(wrap, 0, kt + 1)

    i32 = lambda v: jnp.int32(v)
    pipeline(n_qb * (n_qb - 1) // 2, (i32(1), i32(0)), next_below, False)
    pipeline(n_qb, (i32(0), i32(0)), lambda qb, kt: (qb + 1, kt + 1), True)


def _attention(q, k, vt, bsz, lp):
    pairs = MLA_HEADS // 2
    t = ATTN_T if lp % ATTN_T == 0 else ATTN_CW
    n_qb = lp // t
    return pl.pallas_call(
        functools.partial(_attn_kernel, t=t, n_qb=n_qb),
        grid=(bsz, pairs),
        in_specs=[pl.BlockSpec((lp, 2 * LANES), lambda b, j: (b, j)),
                  pl.BlockSpec((lp, 2 * LANES), lambda b, j: (b, j)),
                  pl.BlockSpec((2 * VT_ROWS, lp), lambda b, j: (j, b))],
        out_specs=pl.BlockSpec((2 * V_HEAD, lp), lambda b, j: (j, b)),
        out_shape=jax.ShapeDtypeStruct((MLA_HEADS * V_HEAD, bsz * lp), bf16),
        scratch_shapes=[pltpu.VMEM((2 * n_qb, 1, t), f32), pltpu.VMEM((2 * n_qb, VT_ROWS, t), f32),
                        pltpu.VMEM((4, t, t), f32), pltpu.VMEM((4, t, t), bf16),
                        pltpu.VMEM((4, 1, t), f32)],
        compiler_params=_params(("parallel", "parallel")),
        name="mla_attention",
    )(q, k, vt)


def _merge_kernel(h_ref, y_ref, at_ref, gs_ref, gm_ref, wso_ref, wo_ref, wout_ref, nf_ref, wr_ref, br_ref,
                  h2_ref, xn_ref, te_ref, tw_ref, tr_ref, cnt_ref, *, tm, lp):
    y_ssd = _dot(y_ref[...], wso_ref[...])
    y_mla = _dot_tn(at_ref[...], wo_ref[...])
    mix = _sigmoid(gs_ref[...].astype(f32)) * y_ssd + _sigmoid(gm_ref[...].astype(f32)) * y_mla
    h2 = h_ref[...] + _dot(mix.astype(bf16), wout_ref[...])
    h2_ref[...] = h2
    xn = _rms(h2, nf_ref[...], EPS)
    _store_token_rows(xn_ref, xn)
    logits = _dot(xn.astype(bf16), wr_ref[...]) + br_ref[...]
    lane = lax.broadcasted_iota(jnp.int32, logits.shape, 1).astype(f32)
    te = jnp.zeros(logits.shape, f32)
    tw = jnp.zeros(logits.shape, f32)
    top = None
    picks = []
    for k in range(TOP_K):
        m = jnp.max(logits, axis=-1, keepdims=True)
        idx = jnp.min(jnp.where(logits == m, lane, float(LANES)), axis=-1, keepdims=True)
        if top is None:
            top = m
        te = jnp.where(lane == float(k), idx, te)
        tw = jnp.where(lane == float(k), jnp.exp(m - top), tw)
        picks.append(lane == idx)
        logits = jnp.where(picks[-1], -jnp.inf, logits)
    tw = tw / jnp.sum(tw, axis=-1, keepdims=True)
    te_ref[...] = te.astype(jnp.int32)
    tw_ref[...] = tw

    row = (lax.broadcasted_iota(jnp.int32, (tm, LANES), 0) + pl.program_id(0) * tm).astype(f32)
    pos = row - lp * jnp.floor((row + 0.5) * (1.0 / lp))
    routed = jnp.where(pos >= FRONT, sum(p.astype(f32) for p in picks), 0.0)
    earlier = (lax.broadcasted_iota(jnp.int32, (tm, tm), 0) > lax.broadcasted_iota(jnp.int32, (tm, tm), 1))
    before = _dot(earlier.astype(bf16), routed.astype(bf16))
    tr = jnp.zeros(logits.shape, f32)
    for k in range(TOP_K):
        tr = jnp.where(lane == float(k), jnp.sum(jnp.where(picks[k], before, 0.0), axis=-1, keepdims=True), tr)
    tr_ref[...] = tr.astype(jnp.int32)
    cnt_ref[...] = jnp.broadcast_to(jnp.sum(routed, axis=0, keepdims=True), cnt_ref.shape).astype(jnp.int32)


def _merge(h, yn, aot, u, wso, wo, wout, norm_ffn, wr, br, tm, lp):
    rows = h.shape[0]
    const = lambda i: (0, 0)
    row = lambda i: (i, 0)
    return pl.pallas_call(
        functools.partial(_merge_kernel, tm=tm, lp=lp),
        grid=(rows // tm,),
        in_specs=[pl.BlockSpec((tm, D_MODEL), row),
                  pl.BlockSpec((tm, D_INNER), row),
                  pl.BlockSpec((MLA_HEADS * V_HEAD, tm), lambda i: (0, i)),
                  pl.BlockSpec((tm, D_MODEL), lambda i: (i, U_GS // D_MODEL)),
                  pl.BlockSpec((tm, D_MODEL), lambda i: (i, U_GM // D_MODEL)),
                  pl.BlockSpec((D_INNER, D_MODEL), const),
                  pl.BlockSpec((D_MODEL, D_MODEL), const),
                  pl.BlockSpec((D_MODEL, D_MODEL), const),
                  pl.BlockSpec((1, D_MODEL), const),
                  pl.BlockSpec((D_MODEL, LANES), const),
                  pl.BlockSpec((1, LANES), const)],
        out_specs=[pl.BlockSpec((tm, D_MODEL), row),
                   pl.BlockSpec((tm * TILE_ROWS, LANES), row),
                   pl.BlockSpec((tm, LANES), row),
                   pl.BlockSpec((tm, LANES), row),
                   pl.BlockSpec((tm, LANES), row),
                   pl.BlockSpec((SUBLANES, LANES), row)],
        out_shape=[jax.ShapeDtypeStruct((rows, D_MODEL), f32),
                   jax.ShapeDtypeStruct((rows * TILE_ROWS, LANES), f32),
                   jax.ShapeDtypeStruct((rows, LANES), jnp.int32),
                   jax.ShapeDtypeStruct((rows, LANES), f32),
                   jax.ShapeDtypeStruct((rows, LANES), jnp.int32),
                   jax.ShapeDtypeStruct((rows // tm * SUBLANES, LANES), jnp.int32)],
        compiler_params=_params(("parallel",)),
        name="merge_route",
    )(h, yn, aot, u, u, wso, wo, wout, norm_ffn, wr, br)


def _deinterleave_kernel(w_ref, g_ref, l_ref, buf):
    buf[...] = w_ref[0].T
    g_ref[0] = buf[pl.ds(0, D_EXPERT, stride=2), :].astype(g_ref.dtype)
    l_ref[0] = buf[pl.ds(1, D_EXPERT, stride=2), :].astype(l_ref.dtype)


def _deinterleave(w1, tk):
    n_e = w1.shape[0]
    out = jax.ShapeDtypeStruct((n_e, D_EXPERT, D_MODEL), bf16)
    return pl.pallas_call(
        _deinterleave_kernel,
        grid=(n_e, D_MODEL // tk),
        in_specs=[pl.BlockSpec((1, tk, 2 * D_EXPERT), lambda e, j: (e, j, 0))],
        out_specs=[pl.BlockSpec((1, D_EXPERT, tk), lambda e, j: (e, 0, j)),
                   pl.BlockSpec((1, D_EXPERT, tk), lambda e, j: (e, 0, j))],
        out_shape=[out, out],
        scratch_shapes=[pltpu.VMEM((2 * D_EXPERT, tk), f32)],
        compiler_params=_params(("parallel", "parallel")),
        name="w1_deinterleave",
    )(w1)


def _tiles_wait(hbm, sem, n_tiles):
    span = hbm.at[pl.ds(0, n_tiles * TILE_ROWS)]
    pltpu.make_async_copy(span, span, sem).wait()


def _dispatch_kernel(dst_ref, fill_ref, x_ref, xs_hbm, sems, *, n_tok, n_fill, n_steps):
    i = pl.program_id(0)
    sem, fsem = sems.at[0], sems.at[1]

    def scatter(r, _):
        src = x_ref.at[pl.ds(pl.multiple_of(r * TILE_ROWS, TILE_ROWS), TILE_ROWS)]
        for k in range(TOP_K):
            row = pl.multiple_of(dst_ref[0, 0, TOP_K * r + k], TILE_ROWS)
            pltpu.make_async_copy(src, xs_hbm.at[pl.ds(row, TILE_ROWS)], sem).start()
        return 0

    lax.fori_loop(0, n_tok, scatter, 0)

    @pl.when(i == n_steps - 1)
    def _():
        def fill(j, _):
            row = pl.multiple_of(fill_ref[0, j], TILE_ROWS)
            pltpu.make_async_copy(x_ref.at[pl.ds(0, TILE_ROWS)], xs_hbm.at[pl.ds(row, TILE_ROWS)], fsem).start()
            return 0

        lax.fori_loop(0, n_fill, fill, 0)
        _tiles_wait(xs_hbm, fsem, n_fill)

    _tiles_wait(xs_hbm, sem, TOP_K * n_tok)


def _dispatch(dst, fill_dst, xn_tiles, n_tiles_out, n_tok):
    n_steps = dst.shape[0]
    n_fill = fill_dst.shape[1]
    return pl.pallas_call(
        functools.partial(_dispatch_kernel, n_tok=n_tok, n_fill=n_fill, n_steps=n_steps),
        grid=(n_steps,),
        in_specs=[pl.BlockSpec((1, 1, TOP_K * n_tok), lambda i: (i, 0, 0), memory_space=pltpu.SMEM),
                  pl.BlockSpec((1, n_fill), lambda i: (0, 0), memory_space=pltpu.SMEM),
                  pl.BlockSpec((n_tok * TILE_ROWS, LANES), lambda i: (i, 0))],
        out_specs=pl.BlockSpec(memory_space=pl.ANY),
        out_shape=jax.ShapeDtypeStruct((n_tiles_out * TILE_ROWS, LANES), f32),
        scratch_shapes=[pltpu.SemaphoreType.DMA((2,))],
        compiler_params=_params(("arbitrary",)),
        name="moe_dispatch",
    )(dst, fill_dst, xn_tiles)


def _moe_kernel(be_ref, nb_ref, x_ref, w1g_ref, w1l_ref, b1g_ref, b1l_ref, w2_ref, b2_ref, o_ref, *, bm):
    i = pl.program_id(0)

    @pl.when(i < nb_ref[0])
    def _():
        xb = _load_token_rows(x_ref, 0, bm).astype(bf16)
        h_glu = _dot_nt(xb, w1g_ref[0]) + b1g_ref[0]
        h_lin = _dot_nt(xb, w1l_ref[0]) + b1l_ref[0]
        h_glu = jnp.minimum(h_glu, SWIGLU_LIMIT)
        h_lin = jnp.clip(h_lin, -SWIGLU_LIMIT, SWIGLU_LIMIT)
        act = h_glu * _sigmoid(SWIGLU_ALPHA * h_glu) * (h_lin + 1.0)
        _store_token_rows(o_ref, _dot(act.astype(bf16), w2_ref[0]) + b2_ref[0])

    @pl.when(i >= nb_ref[0])
    def _():
        o_ref[...] = jnp.zeros_like(o_ref)


def _moe(block_e, n_used, xs, w1g, w1l, b1g, b1l, w2, b2, bm):
    n_blocks = block_e.shape[0]
    wmap = lambda i, be, nb: (be[i], 0, 0)
    grid_spec = pltpu.PrefetchScalarGridSpec(
        num_scalar_prefetch=2,
        grid=(n_blocks,),
        in_specs=[pl.BlockSpec((bm * TILE_ROWS, LANES), lambda i, be, nb: (i, 0)),
                  pl.BlockSpec((1, D_EXPERT, D_MODEL), wmap),
                  pl.BlockSpec((1, D_EXPERT, D_MODEL), wmap),
                  pl.BlockSpec((1, 1, D_EXPERT), wmap),
                  pl.BlockSpec((1, 1, D_EXPERT), wmap),
                  pl.BlockSpec((1, D_EXPERT, D_MODEL), wmap),
                  pl.BlockSpec((1, 1, D_MODEL), wmap)],
        out_specs=pl.BlockSpec((bm * TILE_ROWS, LANES), lambda i, be, nb: (i, 0)),
    )
    return pl.pallas_call(
        functools.partial(_moe_kernel, bm=bm),
        grid_spec=grid_spec,
        out_shape=jax.ShapeDtypeStruct((n_blocks * bm * TILE_ROWS, LANES), f32),
        compiler_params=_params(("arbitrary",)),
        name="moe_experts",
    )(block_e, n_used, xs, w1g, w1l, b1g, b1l, w2, b2)


def _tile_gather(src_hbm, dst, sem, idx_ref, n):
    def body(j, _):
        for u in range(2):
            r = 2 * j + u
            row = pl.multiple_of(idx_ref[r], TILE_ROWS)
            pltpu.make_async_copy(src_hbm.at[pl.ds(row, TILE_ROWS)],
                                  dst.at[pl.ds(pl.multiple_of(r * TILE_ROWS, TILE_ROWS), TILE_ROWS)],
                                  sem).start(priority=u)
        return 0
    lax.fori_loop(0, n // 2, body, 0)


def _combine_kernel(pos_ref, posn_ref, rows_hbm, h_ref, tw_ref, g_ref, o_ref, gbuf, sem, *, tm, n_steps):
    i = pl.program_id(0)
    slot = i % 2
    n = TOP_K * tm

    @pl.when(i == 0)
    def _():
        _tile_gather(rows_hbm, gbuf.at[0], sem.at[0], pos_ref.at[0, 0], n)

    @pl.when(i + 1 < n_steps)
    def _():
        _tile_gather(rows_hbm, gbuf.at[1 - slot], sem.at[1 - slot], posn_ref.at[0, 0], n)

    pltpu.make_async_copy(rows_hbm.at[pl.ds(0, n * TILE_ROWS)], gbuf.at[slot], sem.at[slot]).wait()
    y = h_ref[...]
    tw = tw_ref[...]
    for k in range(TOP_K):
        y = y + tw[:, k:k + 1] * _load_token_rows(gbuf.at[slot], k * tm * TILE_ROWS, tm)
    o_ref[0] = _rms(y, g_ref[...], EPS)


def _combine(pos_tiles, moe_rows, h2, tw, norm_final, bsz, seq, lp, tm):
    tiles_per_seq = seq // tm
    seq_tiles = lp // tm
    n_steps = bsz * tiles_per_seq
    lead = (lp - seq) // tm
    rmap = lambda i: ((i // tiles_per_seq) * seq_tiles + lead + i % tiles_per_seq, 0)
    return pl.pallas_call(
        functools.partial(_combine_kernel, tm=tm, n_steps=n_steps),
        grid=(n_steps,),
        in_specs=[pl.BlockSpec((1, 1, TOP_K * tm), lambda i: (i, 0, 0), memory_space=pltpu.SMEM),
                  pl.BlockSpec((1, 1, TOP_K * tm), lambda i: (jnp.minimum(i + 1, n_steps - 1), 0, 0),
                               memory_space=pltpu.SMEM),
                  pl.BlockSpec(memory_space=pl.ANY),
                  pl.BlockSpec((tm, D_MODEL), rmap),
                  pl.BlockSpec((tm, LANES), rmap),
                  pl.BlockSpec((1, D_MODEL), lambda i: (0, 0))],
        out_specs=pl.BlockSpec((1, tm, D_MODEL), lambda i: (i // tiles_per_seq, i % tiles_per_seq, 0)),
        out_shape=jax.ShapeDtypeStruct((bsz, seq, D_MODEL), f32),
        scratch_shapes=[pltpu.VMEM((2, TOP_K * tm * TILE_ROWS, LANES), f32), pltpu.SemaphoreType.DMA((2,))],
        compiler_params=_params(("arbitrary",)),
        name="moe_combine",
    )(pos_tiles, pos_tiles, moe_rows, h2, tw, norm_final)


def _pack_w_in(w):
    o_dt = D_INNER + CONV_DIM
    o_qa = o_dt + SSD_HEADS
    o_kv = o_qa + Q_LORA
    o_g = o_kv + KV_LORA + QK_ROPE
    zeros = lambda n: jnp.zeros((D_MODEL, n), w.dtype)
    main = jnp.concatenate([
        w[:, :o_dt], w[:, o_g:o_g + 2 * D_MODEL],
        w[:, o_kv:o_g], zeros(LAT_QA - (KV_LORA + QK_ROPE)),
        w[:, o_qa:o_kv], zeros(LAT_W - LAT_QA - Q_LORA)], axis=1).astype(bf16)
    w_dt = jnp.concatenate([w[:, o_dt:o_qa], zeros(LANES - SSD_HEADS)], axis=1).astype(bf16)
    return main, w_dt


def _pad_lanes(v, fill=0.0):
    return jnp.concatenate([v, jnp.full((LANES - v.shape[0],), fill, v.dtype)])[None, :]


def _pack_mla(w_uq, w_ukv):
    hd = QK_NOPE + QK_ROPE
    half = QK_ROPE // 2
    wq = w_uq.reshape(Q_LORA, MLA_HEADS, hd)
    pad = jnp.zeros((Q_LORA, MLA_HEADS, LANES - hd), w_uq.dtype)
    wq_lin = jnp.concatenate([wq, pad], axis=-1)
    wq_swp = jnp.concatenate([jnp.zeros_like(wq[..., :QK_NOPE]), wq[..., QK_NOPE + half:], wq[..., QK_NOPE:QK_NOPE + half],
                              pad], axis=-1)
    wkv = w_ukv.reshape(KV_LORA, MLA_HEADS, QK_NOPE + V_HEAD)
    wk = jnp.concatenate([wkv[..., :QK_NOPE], jnp.zeros((KV_LORA, MLA_HEADS, LANES - QK_NOPE), w_ukv.dtype)], axis=-1)
    wv = jnp.concatenate([wkv[..., QK_NOPE:], jnp.zeros((KV_LORA, MLA_HEADS, VT_ROWS - V_HEAD), w_ukv.dtype)], axis=-1)
    flat = lambda a: a.reshape(a.shape[0], -1).astype(bf16)
    return flat(wq_lin), flat(wq_swp), flat(wk), flat(wv).T


def _rope_tables(lp):
    idx = jnp.arange(lp, dtype=f32)
    pos = jnp.maximum(idx - FRONT, 0.0)
    inv = ROPE_THETA ** (-jnp.arange(0, QK_ROPE, 2, dtype=f32) / QK_ROPE)
    ang = pos[:, None] * inv[None, :]
    cos, sin = jnp.cos(ang), jnp.sin(ang)
    scale = (QK_NOPE + QK_ROPE) ** -0.5 * math.log2(math.e)
    ones = jnp.ones((lp, QK_NOPE), f32)
    z_nope = jnp.zeros((lp, QK_NOPE), f32)
    z_pad = jnp.zeros((lp, LANES - QK_NOPE - QK_ROPE), f32)
    cq = jnp.concatenate([ones, cos, cos, z_pad], axis=1) * scale
    sq = jnp.concatenate([z_nope, -sin, sin, z_pad], axis=1) * scale
    ck = jnp.concatenate([z_nope, cos, cos, z_pad], axis=1)
    sk = jnp.concatenate([z_nope, -sin, sin, z_pad], axis=1)
    lane = jnp.arange(LANES)[None, :]
    qb = (lane == BIAS_LANE).astype(f32)
    kb = jnp.where((lane == BIAS_LANE) & (idx[:, None] < FRONT), NEG, 0.0).astype(f32)
    return cq, sq, ck, sk, qb, kb


def _rope_placement():
    half = QK_ROPE // 2
    r = jnp.arange(QK_ROPE)
    ek = jnp.zeros((QK_ROPE, LANES), f32).at[r, QK_NOPE + r].set(1.0)
    eks = jnp.zeros((QK_ROPE, LANES), f32).at[r, QK_NOPE + (r + half) % QK_ROPE].set(1.0)
    return ek.astype(bf16), eks.astype(bf16)


def _routing_tables(top_e, tile_rank, tile_counts, tm, bm, n_blocks):
    n_rows = n_blocks * bm
    counts = jnp.sum(tile_counts, axis=0)
    padded = ((counts + bm - 1) // bm) * bm
    pends = jnp.cumsum(padded)
    pstarts = pends - padded
    tile_start = pstarts[None, :] + jnp.cumsum(tile_counts, axis=0) - tile_counts
    row_start = jnp.repeat(tile_start, tm, axis=0)
    pick = top_e[:, :, None] == jnp.arange(N_EXPERTS, dtype=jnp.int32)[None, None, :]
    dest = (jnp.sum(jnp.where(pick, row_start[:, None, :], 0), axis=-1) + tile_rank).astype(jnp.int32)
    block_row = jnp.arange(n_blocks, dtype=jnp.int32) * bm
    block_e = jnp.minimum(jnp.sum(pends[None, :] <= block_row[:, None], axis=1), N_EXPERTS - 1).astype(jnp.int32)
    n_used = (pends[-1] // bm).astype(jnp.int32).reshape(1)
    seg_start = jnp.concatenate([pstarts + counts, pends[-1:]])
    seg_len = jnp.concatenate([padded - counts, n_rows - pends[-1:]])
    seg_end = jnp.cumsum(seg_len)
    return dest, block_e, n_used, seg_start, seg_end - seg_len, seg_end


def kernel(x, meta_tokens, norm_mix, w_in, conv_w, conv_b, dt_bias, a_log, d_skip, ssd_norm, w_ssd_out, q_norm, w_uq,
           kv_norm, w_ukv, w_o, w_out, norm_ffn, w_router, b_router, w_mlp1, b_mlp1, w_mlp2, b_mlp2, norm_final):
    bsz, seq, _ = x.shape
    assert seq % CHUNK == 0 and w_in.shape[0] == 1 and TILE_ROWS == SUBLANES
    lp = FRONT + N_META + seq
    n_chunks = lp // CHUNK
    rows = bsz * lp
    tm = 512 if rows % 512 == 0 else CHUNK

    h = jnp.concatenate([jnp.zeros((bsz, FRONT, D_MODEL), x.dtype),
                         jnp.broadcast_to(meta_tokens[None].astype(x.dtype), (bsz, N_META, D_MODEL)), x], axis=1)
    h = h.reshape(rows, D_MODEL)

    w_main, w_dt = _pack_w_in(w_in[0])
    u = _norm_matmul(h, norm_mix, w_main, tm, U_COLS // 2, bf16, "in_proj")
    dt_raw = _norm_matmul(h, norm_mix, w_dt, tm, LANES, f32, "in_proj_dt")

    a_p = _pad_lanes(-jnp.exp(a_log[0].astype(f32)))
    head_of_col = jnp.arange(D_INNER) // SSD_HEADDIM
    e_mat = (jnp.arange(LANES)[:, None] == head_of_col[None, :]).astype(bf16)
    dsk_x = d_skip[0].astype(f32)[head_of_col][None, :]
    yn = _ssd(u, dt_raw, conv_w[0], conv_b, _pad_lanes(dt_bias[0].astype(f32)), a_p, dsk_x, ssd_norm, e_mat,
              bsz, n_chunks)

    wq, wqs, wk, wvt = _pack_mla(w_uq[0], w_ukv[0])
    ek, eks = _rope_placement()
    cq, sq, ck, sk, qb, kb = _rope_tables(lp)
    vb = (jnp.arange(MLA_HEADS * VT_ROWS) % VT_ROWS == ONES_ROW).astype(f32)[:, None]
    q, k, vt = _mla_proj(u, q_norm, kv_norm, wq, wqs, wk, wvt, ek, eks, cq, sq, ck, sk, qb, kb, vb, CHUNK, lp)
    aot = _attention(q, k, vt, bsz, lp)

    wr = jnp.concatenate([w_router[0], jnp.zeros((D_MODEL, LANES - N_EXPERTS), w_router.dtype)], axis=1).astype(bf16)
    br = _pad_lanes(b_router[0].astype(f32), NEG)
    h2, xn_tiles, te, tw, tr, cnt = _merge(h, yn, aot, u, w_ssd_out[0].astype(bf16), w_o[0].astype(bf16),
                                           w_out[0].astype(bf16), norm_ffn, wr, br, tm, lp)

    bm = MOE_BLOCK
    n_real = bsz * (N_META + seq)
    n_blocks = -(-n_real * TOP_K // bm) + N_EXPERTS
    n_rows = n_blocks * bm
    row_id = jnp.arange(rows, dtype=jnp.int32)
    valid = (row_id % lp) >= FRONT
    tile_counts = cnt.reshape(rows // tm, SUBLANES, LANES)[:, 0, :N_EXPERTS]
    dest, block_e, n_used, seg_start, seg_first, seg_end = _routing_tables(
        te[:, :TOP_K], tr[:, :TOP_K], tile_counts, tm, bm, n_blocks)
    pad_ord = (row_id // lp) * FRONT + row_id % lp
    spare = n_rows + pad_ord[:, None] * TOP_K + jnp.arange(TOP_K, dtype=jnp.int32)[None, :]
    n_tok = DISPATCH_TOKENS if rows % DISPATCH_TOKENS == 0 else CHUNK
    dst = (jnp.where(valid[:, None], dest, spare) * TILE_ROWS).reshape(rows // n_tok, 1, TOP_K * n_tok)
    n_fill = n_rows - n_real * TOP_K
    fill_j = jnp.arange(n_fill, dtype=jnp.int32)
    seg = jnp.sum(seg_end[None, :] <= fill_j[:, None], axis=1)
    fill_dst = ((seg_start[seg] + fill_j - seg_first[seg]) * TILE_ROWS).astype(jnp.int32)[None, :]
    xs = _dispatch(dst, fill_dst, xn_tiles, n_rows + bsz * FRONT * TOP_K, n_tok)

    w1g, w1l = _deinterleave(w_mlp1[0], LANES)
    moe_rows = _moe(block_e, n_used, xs, w1g, w1l,
                    b_mlp1[0][:, None, 0::2], b_mlp1[0][:, None, 1::2],
                    w_mlp2[0].astype(bf16), b_mlp2[0][:, None, :], bm)

    tc = CHUNK
    pos = dest.reshape(bsz, lp, TOP_K)[:, lp - seq:, :] * TILE_ROWS
    pos_tiles = pos.reshape(bsz * (seq // tc), tc, TOP_K).transpose(0, 2, 1).reshape(-1, 1, TOP_K * tc)
    return _combine(pos_tiles, moe_rows, h2, tw, norm_final[None, :], bsz, seq, lp, tc)
```
